```python
import jax, jax.numpy as jnp
from jax import lax
import numpy as np

D_MODEL = 4096
BATCH = 4
SEQ = 4096
DEPTH = 2
DEC_BATCH = 8
DEC_SEQ = 32
PAST_LEN = 2048

CHUNK = 64
N_MIXERS = 2
N_CONV_LAYERS = (DEPTH + N_MIXERS - 1) // N_MIXERS
N_ATTN_LAYERS = DEPTH // N_MIXERS
CONV_WIDTH = 31
N_HEADS = 64
N_KV_HEADS = 8
HEAD_DIM = 64
GROUP = N_HEADS // N_KV_HEADS
WINDOW = 128
N_BAND = WINDOW // CHUNK + 1
N_EXPERTS = 32
TOP_K = 4
D_FF = D_MODEL // 2
SWIGLU_LIMIT = 7.0
SWIGLU_ALPHA = 1.702
MOE_BLOCK = 256
EPS = 1e-5
NEG_INF = -1e30

kernel_name = 'chunk_conformer_swa_moe_step'


def rms_norm(x, g):
    x32 = x.astype(jnp.float32)
    y = x32 * lax.rsqrt(jnp.mean(x32 * x32, axis=-1, keepdims=True) + EPS)
    return (y * g.astype(jnp.float32)).astype(x.dtype)


def conv_module(h, hist, pw1_w, pw1_b, dw_w, dw_b, ln_g, ln_b, pw2_w, pw2_b):
    T = h.shape[1]
    u = h @ pw1_w + pw1_b
    a, b = jnp.split(u, 2, axis=-1)
    g = a * jax.nn.sigmoid(b)
    full = jnp.concatenate([hist.astype(g.dtype), g], axis=1)
    y = dw_b
    for k in range(CONV_WIDTH):
        y = y + full[:, k:k + T] * dw_w[k]
    y32 = y.astype(jnp.float32)
    mu = jnp.mean(y32, axis=-1, keepdims=True)
    var = jnp.mean(jnp.square(y32 - mu), axis=-1, keepdims=True)
    yn = ((y32 - mu) * lax.rsqrt(var + EPS)).astype(y.dtype) * ln_g + ln_b
    out = jax.nn.silu(yn) @ pw2_w + pw2_b
    return out, full[:, -(CONV_WIDTH - 1):]


def alibi_slopes():
    return jnp.exp2(-8.0 * jnp.arange(1, N_HEADS + 1, dtype=jnp.float32) / N_HEADS)


def qkv_split(h, qkv_w, qkv_b):
    B, T, _ = h.shape
    qkv = h @ qkv_w + qkv_b
    q = qkv[..., :N_HEADS * HEAD_DIM].reshape(B, T, N_KV_HEADS, GROUP, HEAD_DIM)
    k = qkv[..., N_HEADS * HEAD_DIM:(N_HEADS + N_KV_HEADS) * HEAD_DIM].reshape(B, T, N_KV_HEADS, HEAD_DIM)
    v = qkv[..., (N_HEADS + N_KV_HEADS) * HEAD_DIM:].reshape(B, T, N_KV_HEADS, HEAD_DIM)
    return q, k, v


def band_attention(q, k, v, qpos, kpos, valid, sinks):
    s = jnp.einsum('bnqkgd,bnskd->bnkgqs', q, k).astype(jnp.float32) * (HEAD_DIM ** -0.5)
    slopes = alibi_slopes().reshape(N_KV_HEADS, GROUP)[:, :, None, None]
    dist = jnp.abs(qpos[:, :, None] - kpos[:, None, :]).astype(jnp.float32)[:, None, None]
    s = jnp.where(valid[:, None, None], s - slopes * dist, NEG_INF)
    sink = sinks.astype(jnp.float32).reshape(N_KV_HEADS, GROUP)[:, :, None, None]
    m = jnp.maximum(jnp.max(s, axis=-1, keepdims=True), sink)
    p = jnp.exp(s - m)
    probs = p / (jnp.sum(p, axis=-1, keepdims=True) + jnp.exp(sink - m))
    o = jnp.einsum('bnkgqs,bnskd->bnqkgd', probs.astype(v.dtype), v)
    B, N, Tq = o.shape[:3]
    return o.reshape(B, N * Tq, N_HEADS * HEAD_DIM)


def swa_prompt(h, qkv_w, qkv_b, o_w, o_b, sinks):
    B, S, _ = h.shape
    q, k, v = qkv_split(h, qkv_w, qkv_b)
    nc = S // CHUNK

    def band(t):
        tp = jnp.concatenate([jnp.zeros((B, (N_BAND - 1) * CHUNK) + t.shape[2:], t.dtype), t], axis=1)
        tp = tp.reshape((B, nc + N_BAND - 1, CHUNK) + t.shape[2:])
        return jnp.concatenate([tp[:, j:j + nc] for j in range(N_BAND)], axis=2)

    qpos = jnp.arange(S, dtype=jnp.int32).reshape(nc, CHUNK)
    kpos = ((jnp.arange(nc, dtype=jnp.int32)[:, None] - (N_BAND - 1)) * CHUNK
            + jnp.arange(N_BAND * CHUNK, dtype=jnp.int32)[None, :])
    valid = (kpos >= 0)[:, None, :]
    o = band_attention(q.reshape(B, nc, CHUNK, N_KV_HEADS, GROUP, HEAD_DIM), band(k), band(v),
                       qpos, kpos, valid, sinks)
    return o @ o_w + o_b, k[:, -WINDOW:], v[:, -WINDOW:]


def swa_sample(h, k_hist, v_hist, qkv_w, qkv_b, o_w, o_b, sinks):
    B, T, _ = h.shape
    q, k, v = qkv_split(h, qkv_w, qkv_b)
    w = k_hist.shape[1]
    k_all = jnp.concatenate([k_hist.astype(k.dtype), k], axis=1)
    v_all = jnp.concatenate([v_hist.astype(v.dtype), v], axis=1)
    qpos = jnp.arange(T, dtype=jnp.int32)[None]
    kpos = (jnp.arange(w + T, dtype=jnp.int32) - w)[None]
    valid = jnp.ones((1, 1, w + T), dtype=bool)
    o = band_attention(q[:, None], k_all[:, None], v_all[:, None], qpos, kpos, valid, sinks)
    return o @ o_w + o_b, k_all[:, -w:], v_all[:, -w:]


def moe_ffn(h, router_w, router_b, gu_w, gu_b, dn_w, dn_b):
    T, D = h.shape
    logits = (h @ router_w + router_b).astype(jnp.float32)
    top_v, top_e = lax.top_k(logits, TOP_K)
    top_w = jax.nn.softmax(top_v, axis=-1)
    n = T * TOP_K
    flat_e = top_e.reshape(n).astype(jnp.int32)
    flat_w = top_w.reshape(n)
    flat_tok = jnp.arange(n, dtype=jnp.int32) // TOP_K
    order = jnp.argsort(flat_e)
    se = flat_e[order]
    counts = jnp.bincount(flat_e, length=N_EXPERTS).astype(jnp.int32)
    padded = (counts + MOE_BLOCK - 1) // MOE_BLOCK * MOE_BLOCK
    pad_end = jnp.cumsum(padded)
    pad_start = pad_end - padded
    start = jnp.cumsum(counts) - counts
    dest = pad_start[se] + jnp.arange(n, dtype=jnp.int32) - start[se]
    n_blocks = -(-(n + N_EXPERTS * (MOE_BLOCK - 1)) // MOE_BLOCK)
    n_pad = n_blocks * MOE_BLOCK
    row_tok = jnp.full((n_pad,), T, jnp.int32).at[dest].set(flat_tok[order])
    row_w = jnp.zeros((n_pad,), jnp.float32).at[dest].set(flat_w[order])
    blk_start = jnp.arange(n_blocks, dtype=jnp.int32) * MOE_BLOCK
    blk_e = jnp.minimum(jnp.searchsorted(pad_end, blk_start, side='right'), N_EXPERTS - 1)
    h_pad = jnp.concatenate([h, jnp.zeros((1, D), h.dtype)], axis=0)

    def body(acc, xs):
        e, tok, w = xs
        xb = h_pad[tok]
        gu = xb @ gu_w[e] + gu_b[e]
        gate, up = gu[:, :D_FF], gu[:, D_FF:]
        gate = jnp.minimum(gate, SWIGLU_LIMIT)
        up = jnp.clip(up, -SWIGLU_LIMIT, SWIGLU_LIMIT)
        act = (up + 1) * (gate * jax.nn.sigmoid(SWIGLU_ALPHA * gate))
        yb = act @ dn_w[e] + dn_b[e]
        acc = acc.at[tok].add(yb.astype(jnp.float32) * w[:, None])
        return acc, None

    acc, _ = lax.scan(body, jnp.zeros((T + 1, D), jnp.float32),
                      (blk_e, row_tok.reshape(n_blocks, MOE_BLOCK), row_w.reshape(n_blocks, MOE_BLOCK)))
    return acc[:T].astype(h.dtype)


def trunk(x, c, conv_state, k_cache, v_cache, prompt, ada_w, ada_b, norm_mix, norm_ffn, norm_final,
          conv_p, attn_p, moe_p):
    B, T, D = x.shape
    pw1_w, pw1_b, dw_w, dw_b, ln_g, ln_b, pw2_w, pw2_b = conv_p
    qkv_w, qkv_b, o_w, o_b, sinks = attn_p
    router_w, router_b, gu_w, gu_b, dn_w, dn_b = moe_p
    sc = jax.nn.silu(c)
    new_conv, new_k, new_v = [], [], []
    for i in range(DEPTH):
        mod = (sc @ ada_w[i] + ada_b[i])[:, None, :]
        sh1, s1, g1, sh2, s2, g2 = jnp.split(mod, 6, axis=-1)
        h = rms_norm(x, norm_mix[i]) * (1 + s1) + sh1
        j = i // N_MIXERS
        if i % N_MIXERS == 0:
            hist = jnp.zeros((B, CONV_WIDTH - 1, D), x.dtype) if prompt else conv_state[j]
            out, st = conv_module(h, hist, pw1_w[j], pw1_b[j], dw_w[j], dw_b[j], ln_g[j], ln_b[j],
                                  pw2_w[j], pw2_b[j])
            new_conv.append(st)
        else:
            if prompt:
                out, kst, vst = swa_prompt(h, qkv_w[j], qkv_b[j], o_w[j], o_b[j], sinks[j])
            else:
                out, kst, vst = swa_sample(h, k_cache[j], v_cache[j], qkv_w[j], qkv_b[j], o_w[j],
                                           o_b[j], sinks[j])
            new_k.append(kst)
            new_v.append(vst)
        x = x + g1 * out
        h = rms_norm(x, norm_ffn[i]) * (1 + s2) + sh2
        f = moe_ffn(h.reshape(B * T, D), router_w[i], router_b[i], gu_w[i], gu_b[i], dn_w[i], dn_b[i])
        x = x + g2 * f.reshape(B, T, D)
    return rms_norm(x, norm_final), jnp.stack(new_conv), jnp.stack(new_k), jnp.stack(new_v)


def setup_inputs(seed: int = 0) -> dict:
    key = jax.random.key(seed)
    ks = iter(jax.random.split(key, 40))

    def nrm(shape, scale):
        return jax.random.normal(next(ks), shape, jnp.float32) * scale

    D = D_MODEL
    wc = min(WINDOW, PAST_LEN)
    qkv_dim = (N_HEADS + 2 * N_KV_HEADS) * HEAD_DIM
    return {
        'x_prompt': nrm((BATCH, SEQ, D), 1.0),
        'x_sample': nrm((DEC_BATCH, DEC_SEQ, D), 1.0),
        'c_prompt': nrm((BATCH, D), 1.0),
        'c_sample': nrm((DEC_BATCH, D), 1.0),
        'state_conv': nrm((N_CONV_LAYERS, DEC_BATCH, CONV_WIDTH - 1, D), 0.5),
        'cache_k': nrm((N_ATTN_LAYERS, DEC_BATCH, wc, N_KV_HEADS, HEAD_DIM), 1.0),
        'cache_v': nrm((N_ATTN_LAYERS, DEC_BATCH, wc, N_KV_HEADS, HEAD_DIM), 1.0),
        'ada_w': nrm((DEPTH, D, 6 * D), 0.5 * D ** -0.5),
        'ada_b': nrm((DEPTH, 6 * D), 0.02),
        'norm_mix': 1.0 + nrm((DEPTH, D), 0.02),
        'norm_ffn': 1.0 + nrm((DEPTH, D), 0.02),
        'norm_final': 1.0 + nrm((D,), 0.02),
        'conv_pw1_w': nrm((N_CONV_LAYERS, D, 2 * D), D ** -0.5),
        'conv_pw1_b': nrm((N_CONV_LAYERS, 2 * D), 0.02),
        'conv_dw_w': nrm((N_CONV_LAYERS, CONV_WIDTH, D), CONV_WIDTH ** -0.5),
        'conv_dw_b': nrm((N_CONV_LAYERS, D), 0.02),
        'conv_ln_g': 1.0 + nrm((N_CONV_LAYERS, D), 0.02),
        'conv_ln_b': nrm((N_CONV_LAYERS, D), 0.02),
        'conv_pw2_w': nrm((N_CONV_LAYERS, D, D), D ** -0.5),
        'conv_pw2_b': nrm((N_CONV_LAYERS, D), 0.02),
        'attn_qkv_w': nrm((N_ATTN_LAYERS, D, qkv_dim), D ** -0.5),
        'attn_qkv_b': nrm((N_ATTN_LAYERS, qkv_dim), 0.02),
        'attn_o_w': nrm((N_ATTN_LAYERS, N_HEADS * HEAD_DIM, D), (N_HEADS * HEAD_DIM) ** -0.5),
        'attn_o_b': nrm((N_ATTN_LAYERS, D), 0.02),
        'attn_sinks': nrm((N_ATTN_LAYERS, N_HEADS), 1.0),
        'router_w': nrm((DEPTH, D, N_EXPERTS), D ** -0.5),
        'router_b': nrm((DEPTH, N_EXPERTS), 0.01),
        'moe_gu_w': nrm((DEPTH, N_EXPERTS, D, 2 * D_FF), D ** -0.5),
        'moe_gu_b': nrm((DEPTH, N_EXPERTS, 2 * D_FF), 0.02),
        'moe_dn_w': nrm((DEPTH, N_EXPERTS, D_FF, D), D_FF ** -0.5),
        'moe_dn_b': nrm((DEPTH, N_EXPERTS, D), 0.02),
    }


def reference(x_prompt, x_sample, c_prompt, c_sample, state_conv, cache_k, cache_v,
              ada_w, ada_b, norm_mix, norm_ffn, norm_final,
              conv_pw1_w, conv_pw1_b, conv_dw_w, conv_dw_b, conv_ln_g, conv_ln_b, conv_pw2_w, conv_pw2_b,
              attn_qkv_w, attn_qkv_b, attn_o_w, attn_o_b, attn_sinks,
              router_w, router_b, moe_gu_w, moe_gu_b, moe_dn_w, moe_dn_b):
    conv_p = (conv_pw1_w, conv_pw1_b, conv_dw_w, conv_dw_b, conv_ln_g, conv_ln_b, conv_pw2_w, conv_pw2_b)
    attn_p = (attn_qkv_w, attn_qkv_b, attn_o_w, attn_o_b, attn_sinks)
    moe_p = (router_w, router_b, moe_gu_w, moe_gu_b, moe_dn_w, moe_dn_b)
    y_prompt, conv_prompt, k_prompt, v_prompt = trunk(
        x_prompt, c_prompt, None, None, None, True, ada_w, ada_b, norm_mix, norm_ffn, norm_final,
        conv_p, attn_p, moe_p)
    y_sample, conv_sample, k_sample, v_sample = trunk(
        x_sample, c_sample, state_conv, cache_k, cache_v, False, ada_w, ada_b, norm_mix, norm_ffn,
        norm_final, conv_p, attn_p, moe_p)
    return (y_prompt, y_sample, conv_prompt, conv_sample, k_prompt, v_prompt, k_sample, v_sample)
```

```python
import functools

import jax
import jax.numpy as jnp
from jax import lax
from jax.experimental import pallas as pl
from jax.experimental.pallas import tpu as pltpu

CHUNK = 64
WINDOW = 128
CONV_WIDTH = 31
TOP_K = 4
SWIGLU_LIMIT = 7.0
SWIGLU_ALPHA = 1.702
EPS = 1e-5
NEG_INF = -1e30

LANES = 128
STREAM_ROWS_PAD = 16
VMEM_LIMIT_BYTES = 56 * 1024 * 1024
ROW_TILE = 256
MM_ROW_CAP = 1536
MM_COL_TILE = 512
MOE_BLOCK = 256
MOE_GU_COL_TILE = 512
MOE_DN_COL_TILE = 1024
CONV_ROWS = 32
CONV_COL_CHUNK = 512
ATTN_ROWS = 256
COMBINE_ROWS = 64
ADA_COL_TILE = 768

f32 = jnp.float32
bf16 = jnp.bfloat16


def _pick(total, cap, mult):
    best = None
    for t in range(mult, min(total, cap) + 1, mult):
        if total % t == 0:
            best = t
    assert best is not None, (total, cap, mult)
    return best


def _params(*sem):
    return pltpu.CompilerParams(dimension_semantics=sem, vmem_limit_bytes=VMEM_LIMIT_BYTES)


def _dot(a, b):
    return jnp.dot(a, b, preferred_element_type=f32)


def _ada_kernel(c_ref, w_ref, b_ref, o_ref):
    c = c_ref[...]
    s = (c * jax.nn.sigmoid(c)).astype(bf16)
    o_ref[...] = _dot(s, w_ref[...].astype(bf16)) + b_ref[...]


def _ada(c_pad, ada_w, ada_b):
    L, D, N = ada_w.shape
    R = c_pad.shape[0]
    tn = _pick(N, ADA_COL_TILE, LANES)
    return pl.pallas_call(
        _ada_kernel,
        grid=(L, N // tn),
        in_specs=[
            pl.BlockSpec((R, D), lambda l, n: (0, 0)),
            pl.BlockSpec((None, D, tn), lambda l, n: (l, 0, n)),
            pl.BlockSpec((None, 1, tn), lambda l, n: (l, 0, n)),
        ],
        out_specs=pl.BlockSpec((None, R, tn), lambda l, n: (l, 0, n)),
        out_shape=jax.ShapeDtypeStruct((L, R, N), f32),
        compiler_params=_params("arbitrary", "arbitrary"),
        name="ada",
    )(c_pad, ada_w, ada_b.reshape(L, 1, N))


def _modulated_norm(x, nw, scale, shift, rg):
    t, d = x.shape
    y = x * lax.rsqrt(jnp.mean(x * x, axis=-1, keepdims=True) + EPS) * nw
    h = y.reshape(t // rg, rg, d) * (1.0 + scale) + shift
    return h.reshape(t, d)


def _norm_kernel(x_ref, nw_ref, sc_ref, sh_ref, h_ref, *, rg):
    h = _modulated_norm(x_ref[...], nw_ref[...], sc_ref[...], sh_ref[...], rg)
    h_ref[...] = h.astype(h_ref.dtype)


def _norm_mod(x, nw, mod_g, shift_col, scale_col, rg, tile):
    T, D = x.shape
    gt = tile // rg
    return pl.pallas_call(
        functools.partial(_norm_kernel, rg=rg),
        grid=(T // tile,),
        in_specs=[
            pl.BlockSpec((tile, D), lambda i: (i, 0)),
            pl.BlockSpec((1, D), lambda i: (0, 0)),
            pl.BlockSpec((gt, 1, D), lambda i: (i, 0, scale_col)),
            pl.BlockSpec((gt, 1, D), lambda i: (i, 0, shift_col)),
        ],
        out_specs=pl.BlockSpec((tile, D), lambda i: (i, 0)),
        out_shape=jax.ShapeDtypeStruct((T, D), bf16),
        compiler_params=_params("arbitrary"),
        name="norm_mod",
    )(x, nw.reshape(1, D), mod_g, mod_g)


def _pack_bf16_pairs(h):
    d = h.shape[1]
    bits = pltpu.bitcast(h.astype(bf16).astype(f32), jnp.uint32)
    return (bits[:, : d // 2] >> 16) | (bits[:, d // 2:] & jnp.uint32(0xFFFF0000))


def _unpack_bf16_pairs(w):
    lo = pltpu.bitcast(w << 16, f32).astype(bf16)
    hi = pltpu.bitcast(w & jnp.uint32(0xFFFF0000), f32).astype(bf16)
    return lo, hi


def _norm_route_kernel(x_ref, nw_ref, sc_ref, sh_ref, rw_ref, rb_ref,
                       hp_ref, topi_ref, topw_ref, sel_ref, cnt_ref, *, rg, top_k):
    h = _modulated_norm(x_ref[...], nw_ref[...], sc_ref[...], sh_ref[...], rg)
    hp_ref[...] = _pack_bf16_pairs(h)
    logits = _dot(h.astype(bf16), rw_ref[...]) + rb_ref[...]
    t, e = logits.shape
    lane = lax.broadcasted_iota(jnp.int32, (t, e), 1).astype(f32)
    kcol = lax.broadcasted_iota(jnp.int32, (t, top_k), 1)
    sel = jnp.zeros((t, e), f32)
    topi = jnp.zeros((t, top_k), f32)
    topv = jnp.zeros((t, top_k), f32)
    work = logits
    for k in range(top_k):
        m = jnp.max(work, axis=-1, keepdims=True)
        idx = jnp.min(jnp.where(work == m, lane, float(e)), axis=-1, keepdims=True)
        hit = lane == idx
        sel = sel + hit.astype(f32)
        topi = jnp.where(kcol == k, idx, topi)
        topv = jnp.where(kcol == k, m, topv)
        work = jnp.where(hit, -jnp.inf, work)
    p = jnp.exp(topv - topv[:, 0:1])
    topw_ref[...] = p / jnp.sum(p, axis=-1, keepdims=True)
    topi_ref[...] = topi.astype(jnp.int32)
    sel_ref[...] = sel.astype(sel_ref.dtype)
    cnt_ref[...] = jnp.sum(sel, axis=0, keepdims=True)


def _norm_route(x, nw, mod_g, shift_col, scale_col, rw, rb, rg, tile):
    T, D = x.shape
    E = rw.shape[1]
    gt = tile // rg
    nt = T // tile
    return pl.pallas_call(
        functools.partial(_norm_route_kernel, rg=rg, top_k=TOP_K),
        grid=(nt,),
        in_specs=[
            pl.BlockSpec((tile, D), lambda i: (i, 0)),
            pl.BlockSpec((1, D), lambda i: (0, 0)),
            pl.BlockSpec((gt, 1, D), lambda i: (i, 0, scale_col)),
            pl.BlockSpec((gt, 1, D), lambda i: (i, 0, shift_col)),
            pl.BlockSpec((D, E), lambda i: (0, 0)),
            pl.BlockSpec((1, E), lambda i: (0, 0)),
        ],
        out_specs=[
            pl.BlockSpec((tile, D // 2), lambda i: (i, 0)),
            pl.BlockSpec((tile, TOP_K), lambda i: (i, 0)),
            pl.BlockSpec((tile, TOP_K), lambda i: (i, 0)),
            pl.BlockSpec((tile, E), lambda i: (i, 0)),
            pl.BlockSpec((None, 1, E), lambda i: (i, 0, 0)),
        ],
        out_shape=[
            jax.ShapeDtypeStruct((T, D // 2), jnp.uint32),
            jax.ShapeDtypeStruct((T, TOP_K), jnp.int32),
            jax.ShapeDtypeStruct((T, TOP_K), f32),
            jax.ShapeDtypeStruct((T, E), bf16),
            jax.ShapeDtypeStruct((nt, 1, E), f32),
        ],
        compiler_params=_params("arbitrary"),
        name="norm_route",
    )(x, nw.reshape(1, D), mod_g, mod_g, rw.astype(bf16), rb.reshape(1, E))


def _final_norm_kernel(x_ref, nw_ref, o_ref):
    x = x_ref[...]
    o_ref[...] = x * lax.rsqrt(jnp.mean(x * x, axis=-1, keepdims=True) + EPS) * nw_ref[...]


def _final_norm(x, nw, tile):
    T, D = x.shape
    return pl.pallas_call(
        _final_norm_kernel,
        grid=(T // tile,),
        in_specs=[pl.BlockSpec((tile, D), lambda i: (i, 0)), pl.BlockSpec((1, D), lambda i: (0, 0))],
        out_specs=pl.BlockSpec((tile, D), lambda i: (i, 0)),
        out_shape=jax.ShapeDtypeStruct((T, D), f32),
        compiler_params=_params("arbitrary"),
        name="final_norm",
    )(x, nw.reshape(1, D))


def _glu_kernel(x_ref, wa_ref, wb_ref, ba_ref, bb_ref, o_ref):
    x = x_ref[...]
    a = _dot(x, wa_ref[...]) + ba_ref[...]
    b = _dot(x, wb_ref[...]) + bb_ref[...]
    o_ref[...] = a * jax.nn.sigmoid(b)


def _glu_matmul(x, w, b):
    T, K = x.shape
    N = w.shape[1] // 2
    tm = _pick(T, MM_ROW_CAP, 32)
    tn = _pick(N, MM_COL_TILE, LANES)
    nb = N // tn
    b2 = b.reshape(1, 2 * N)
    return pl.pallas_call(
        _glu_kernel,
        grid=(T // tm, nb),
        in_specs=[
            pl.BlockSpec((tm, K), lambda m, n: (m, 0)),
            pl.BlockSpec((K, tn), lambda m, n: (0, n)),
            pl.BlockSpec((K, tn), lambda m, n: (0, n + nb)),
            pl.BlockSpec((1, tn), lambda m, n: (0, n)),
            pl.BlockSpec((1, tn), lambda m, n: (0, n + nb)),
        ],
        out_specs=pl.BlockSpec((tm, tn), lambda m, n: (m, n)),
        out_shape=jax.ShapeDtypeStruct((T, N), f32),
        compiler_params=_params("arbitrary", "arbitrary"),
        name="pw1_glu",
    )(x, w, w, b2, b2)


def _linear_kernel(x_ref, w_ref, b_ref, o_ref):
    o_ref[...] = (_dot(x_ref[...], w_ref[...]) + b_ref[...]).astype(o_ref.dtype)


def _linear_grouped_out(x, w, b, groups, out_dtype, name):
    T, K = x.shape
    ng = w.shape[1] // groups
    tm = _pick(T, MM_ROW_CAP, 32)
    return pl.pallas_call(
        _linear_kernel,
        grid=(T // tm, groups),
        in_specs=[
            pl.BlockSpec((tm, K), lambda m, n: (m, 0)),
            pl.BlockSpec((K, ng), lambda m, n: (0, n)),
            pl.BlockSpec((1, ng), lambda m, n: (0, n)),
        ],
        out_specs=pl.BlockSpec((None, tm, ng), lambda m, n: (n, m, 0)),
        out_shape=jax.ShapeDtypeStruct((groups, T, ng), out_dtype),
        compiler_params=_params("arbitrary", "arbitrary"),
        name=name,
    )(x, w, b.reshape(1, -1))


def _linear(x, w, b, out_dtype, name):
    T, K = x.shape
    N = w.shape[1]
    tm = _pick(T, MM_ROW_CAP, 32)
    tn = _pick(N, MM_COL_TILE, LANES)
    return pl.pallas_call(
        _linear_kernel,
        grid=(T // tm, N // tn),
        in_specs=[
            pl.BlockSpec((tm, K), lambda m, n: (m, 0)),
            pl.BlockSpec((K, tn), lambda m, n: (0, n)),
            pl.BlockSpec((1, tn), lambda m, n: (0, n)),
        ],
        out_specs=pl.BlockSpec((tm, tn), lambda m, n: (m, n)),
        out_shape=jax.ShapeDtypeStruct((T, N), out_dtype),
        compiler_params=_params("arbitrary", "arbitrary"),
        name=name,
    )(x, w, b.reshape(1, -1))


def _residual_kernel(l_ref, w_ref, b_ref, xr_ref, g_ref, o_ref, *, rg):
    groups = l_ref.shape[0]
    acc = _dot(l_ref[0], w_ref[0])
    for g in range(1, groups):
        acc = acc + _dot(l_ref[g], w_ref[g])
    y = acc + b_ref[...]
    tm, tn = y.shape
    upd = (y.reshape(tm // rg, rg, tn) * g_ref[...]).reshape(tm, tn)
    o_ref[...] = xr_ref[...] + upd


def _residual_matmul(lhs, w, b, x_res, mod_g, gate_col, rg, name):
    G, T, Kg = lhs.shape
    N = w.shape[2]
    tm = _pick(T, MM_ROW_CAP, rg)
    tn = _pick(N, MM_COL_TILE, LANES)
    nb = N // tn
    return pl.pallas_call(
        functools.partial(_residual_kernel, rg=rg),
        grid=(T // tm, nb),
        in_specs=[
            pl.BlockSpec((G, tm, Kg), lambda m, n: (0, m, 0)),
            pl.BlockSpec((G, Kg, tn), lambda m, n: (0, 0, n)),
            pl.BlockSpec((1, tn), lambda m, n: (0, n)),
            pl.BlockSpec((tm, tn), lambda m, n: (m, n)),
            pl.BlockSpec((tm // rg, 1, tn), lambda m, n: (m, 0, gate_col * nb + n)),
        ],
        out_specs=pl.BlockSpec((tm, tn), lambda m, n: (m, n)),
        out_shape=jax.ShapeDtypeStruct((T, N), f32),
        compiler_params=_params("arbitrary", "arbitrary"),
        name=name,
    )(lhs, w, b.reshape(1, N), x_res, mod_g)


def _conv_kernel(g_ref, prev_ref, hist_ref, w_ref, b_ref, lg_ref, lb_ref, o_ref, buf, ybuf, *,
                 prompt_steps, steps_per_seq, col_chunk):
    tt, d = g_ref.shape
    hb = prev_ref.shape[0]
    taps = w_ref.shape[0]
    i = pl.program_id(0)
    prev = jnp.where((i % steps_per_seq) == 0, 0.0, prev_ref[...])
    buf[0:hb, :] = jnp.where(i >= prompt_steps, hist_ref[...], prev)
    buf[hb:hb + tt, :] = g_ref[...]
    off = hb - (taps - 1)
    for c in range(d // col_chunk):
        cs = slice(c * col_chunk, (c + 1) * col_chunk)
        acc = jnp.broadcast_to(b_ref[:, cs], (tt, col_chunk))
        for k in range(taps):
            acc = acc + buf[off + k:off + k + tt, cs] * w_ref[k:k + 1, cs]
        ybuf[:, cs] = acc
    y = ybuf[...]
    mu = jnp.mean(y, axis=-1, keepdims=True)
    yc = y - mu
    var = jnp.mean(yc * yc, axis=-1, keepdims=True)
    yn = yc * lax.rsqrt(var + EPS) * lg_ref[...] + lb_ref[...]
    o_ref[...] = (yn * jax.nn.sigmoid(yn)).astype(o_ref.dtype)


def _conv_module(g, hist, dw_w, dw_b, ln_g, ln_b, n_prompt_rows, seq):
    T, D = g.shape
    tt = CONV_ROWS
    taps = dw_w.shape[0]
    prompt_steps = n_prompt_rows // tt
    last_hist = hist.shape[0] // tt - 1
    kern = functools.partial(_conv_kernel, prompt_steps=prompt_steps, steps_per_seq=seq // tt,
                             col_chunk=_pick(D, CONV_COL_CHUNK, LANES))
    vec = pl.BlockSpec((1, D), lambda i: (0, 0))
    return pl.pallas_call(
        kern,
        grid=(T // tt,),
        in_specs=[
            pl.BlockSpec((tt, D), lambda i: (i, 0)),
            pl.BlockSpec((tt, D), lambda i: (jnp.maximum(i - 1, 0), 0)),
            pl.BlockSpec((tt, D), lambda i: (jnp.clip(i - prompt_steps, 0, last_hist), 0)),
            pl.BlockSpec((taps, D), lambda i: (0, 0)),
            vec, vec, vec,
        ],
        out_specs=pl.BlockSpec((tt, D), lambda i: (i, 0)),
        out_shape=jax.ShapeDtypeStruct((T, D), bf16),
        scratch_shapes=[pltpu.VMEM((2 * tt, D), f32), pltpu.VMEM((tt, D), f32)],
        compiler_params=_params("arbitrary"),
        name="conv_ln_swish",
    )(g, g, hist, dw_w, dw_b.reshape(1, D), ln_g.reshape(1, D), ln_b.reshape(1, D))


def _attn_prompt_kernel(*refs, live_tiles, **kw):
    o_ref = refs[7]
    live = pl.program_id(1) < live_tiles

    @pl.when(live)
    def _():
        _attn_kernel(*refs, **kw)

    @pl.when(jnp.logical_not(live))
    def _():
        o_ref[...] = jnp.zeros_like(o_ref)


def _attn_kernel(*refs, n_chunks, cq, has_halo, scale, chunks_per_seq):
    if has_halo:
        q_ref, km_ref, kh_ref, vm_ref, vh_ref, bias_ref, sink_ref, o_ref, kbuf, vbuf = refs
        hr = kh_ref.shape[0]
        kbuf[0:hr, :] = kh_ref[...]
        vbuf[0:hr, :] = vh_ref[...]
    else:
        q_ref, km_ref, vm_ref, bias_ref, sink_ref, o_ref, kbuf, vbuf = refs
        hr = 0
    mr = km_ref.shape[0]
    kbuf[hr:hr + mr, :] = km_ref[...]
    vbuf[hr:hr + mr, :] = vm_ref[...]
    G = q_ref.shape[0]
    pw = q_ref.shape[2]
    hd = pw // 2
    nb = bias_ref.shape[2]
    rows = G * cq
    lane_k = lax.broadcasted_iota(jnp.int32, (nb, pw), 1)
    lane_o = lax.broadcasted_iota(jnp.int32, (rows, pw), 1)
    col = lax.broadcasted_iota(jnp.int32, (rows, nb), 1)
    first_chunk = (pl.program_id(1) % (chunks_per_seq // n_chunks)) * n_chunks if has_halo else None
    for j in range(n_chunks):
        q = q_ref[:, j * cq:(j + 1) * cq, :].reshape(rows, pw)
        kp = kbuf[j * cq:j * cq + nb, :]
        vp = vbuf[j * cq:j * cq + nb, :].astype(bf16)
        outs = []
        for hh in range(2):
            in_head = (lane_k < hd) if hh == 0 else (lane_k >= hd)
            kh = jnp.where(in_head, kp, 0.0).astype(bf16)
            s = lax.dot_general(q, kh, (((1,), (1,)), ((), ())), preferred_element_type=f32)
            s = s * scale + bias_ref[hh]
            if has_halo:
                kpos0 = (first_chunk + j) * cq - (nb - cq)
                s = jnp.where(col + kpos0 >= 0, s, NEG_INF)
            sink = sink_ref[hh]
            m = jnp.maximum(jnp.max(s, axis=-1, keepdims=True), sink)
            p = jnp.exp(s - m)
            den = jnp.sum(p, axis=-1, keepdims=True) + jnp.exp(sink - m)
            o = _dot(p.astype(bf16), vp)
            outs.append(o / den)
        o_pair = jnp.where(lane_o < hd, outs[0], outs[1])
        o_ref[:, j * cq:(j + 1) * cq, :] = o_pair.reshape(G, cq, pw).astype(o_ref.dtype)


def _attn_tables(sinks, n_kv, group, cq, nb, window):
    n_heads = n_kv * group
    slopes = jnp.exp2(-8.0 * jnp.arange(1, n_heads + 1, dtype=f32) / n_heads).reshape(n_kv, group)
    qi = jnp.arange(cq, dtype=jnp.int32)[:, None]
    si = jnp.arange(nb, dtype=jnp.int32)[None, :]
    dist = jnp.abs(window + qi - si).astype(f32)
    bias = -slopes[:, :, None, None] * dist[None, None]
    bias = bias.reshape(n_kv, group * cq, nb)
    sink = jnp.broadcast_to(sinks.astype(f32).reshape(n_kv, group, 1, 1), (n_kv, group, cq, 1))
    return bias, sink.reshape(n_kv, group * cq, 1)


def _attn_prompt(q_r, kv, sinks, n_kv, hd, seq, n_rows, window, chunk):
    G, T, Q = q_r.shape
    pw = 2 * hd
    pairs = n_kv // 2
    tr = _pick(seq, ATTN_ROWS, max(chunk, window))
    n_chunks = tr // chunk
    nb = window + chunk
    bias, sink = _attn_tables(sinks, n_kv, G, chunk, nb, window)
    hpb = tr // window
    assert T % tr == 0 and n_rows % tr == 0
    kern = functools.partial(_attn_prompt_kernel, live_tiles=n_rows // tr, n_chunks=n_chunks, cq=chunk,
                             has_halo=True, scale=hd ** -0.5, chunks_per_seq=seq // chunk)
    halo = lambda off: (lambda p, i: (jnp.maximum(i * hpb - 1, 0), off + p))
    return pl.pallas_call(
        kern,
        grid=(pairs, T // tr),
        in_specs=[
            pl.BlockSpec((G, tr, pw), lambda p, i: (0, i, p)),
            pl.BlockSpec((tr, pw), lambda p, i: (i, p)),
            pl.BlockSpec((window, pw), halo(0)),
            pl.BlockSpec((tr, pw), lambda p, i: (i, pairs + p)),
            pl.BlockSpec((window, pw), halo(pairs)),
            pl.BlockSpec((2, G * chunk, nb), lambda p, i: (p, 0, 0)),
            pl.BlockSpec((2, G * chunk, 1), lambda p, i: (p, 0, 0)),
        ],
        out_specs=pl.BlockSpec((G, tr, pw), lambda p, i: (0, i, p)),
        out_shape=jax.ShapeDtypeStruct((G, T, Q), bf16),
        scratch_shapes=[pltpu.VMEM((window + tr, pw), f32), pltpu.VMEM((window + tr, pw), f32)],
        compiler_params=_params("arbitrary", "arbitrary"),
        name="attn_prompt",
    )(q_r, kv, kv, kv, kv, bias, sink)


def _attn_sample(q_r, kv_all, sinks, o_prev, n_kv, hd, t_new, row0, n_streams):
    G, T, Q = q_r.shape
    pw = 2 * hd
    pairs = n_kv // 2
    nb = kv_all.shape[0] // n_streams
    bias, sink = _attn_tables(sinks, n_kv, G, t_new, nb, nb - t_new)
    kern = functools.partial(_attn_kernel, n_chunks=1, cq=t_new, has_halo=False,
                             scale=hd ** -0.5, chunks_per_seq=1)
    blk0 = row0 // t_new
    return pl.pallas_call(
        lambda *refs: kern(*refs[:5], *refs[6:]),
        grid=(pairs, n_streams),
        in_specs=[
            pl.BlockSpec((G, t_new, pw), lambda p, i: (0, blk0 + i, p)),
            pl.BlockSpec((nb, pw), lambda p, i: (i, p)),
            pl.BlockSpec((nb, pw), lambda p, i: (i, pairs + p)),
            pl.BlockSpec((2, G * t_new, nb), lambda p, i: (p, 0, 0)),
            pl.BlockSpec((2, G * t_new, 1), lambda p, i: (p, 0, 0)),
            pl.BlockSpec(memory_space=pl.ANY),
        ],
        out_specs=pl.BlockSpec((G, t_new, pw), lambda p, i: (0, blk0 + i, p)),
        out_shape=jax.ShapeDtypeStruct((G, T, Q), bf16),
        scratch_shapes=[pltpu.VMEM((nb, pw), f32), pltpu.VMEM((nb, pw), f32)],
        input_output_aliases={5: 0},
        compiler_params=_params("arbitrary", "arbitrary"),
        name="attn_sample",
    )(q_r, kv_all, kv_all, bias, sink, o_prev)


def _pos_kernel(sel_ref, topi_ref, base_ref, dest_ref):
    t, e = sel_ref.shape
    top_k = topi_ref.shape[1]
    r = lax.broadcasted_iota(jnp.int32, (t, t), 0)
    c = lax.broadcasted_iota(jnp.int32, (t, t), 1)
    earlier = (r > c).astype(bf16)
    rank = _dot(earlier, sel_ref[...]) + base_ref[...]
    lane = lax.broadcasted_iota(jnp.int32, (t, e), 1)
    kcol = lax.broadcasted_iota(jnp.int32, (t, top_k), 1)
    topi = topi_ref[...]
    dest = jnp.zeros((t, top_k), f32)
    for k in range(top_k):
        dk = jnp.sum(jnp.where(lane == topi[:, k:k + 1], rank, 0.0), axis=-1, keepdims=True)
        dest = jnp.where(kcol == k, dk, dest)
    dest_ref[...] = dest.astype(jnp.int32)


def _positions(sel, topi, base, tile):
    T, E = sel.shape
    K = topi.shape[1]
    return pl.pallas_call(
        _pos_kernel,
        grid=(T // tile,),
        in_specs=[
            pl.BlockSpec((tile, E), lambda i: (i, 0)),
            pl.BlockSpec((tile, K), lambda i: (i, 0)),
            pl.BlockSpec((None, 1, E), lambda i: (i, 0, 0)),
        ],
        out_specs=pl.BlockSpec((tile, K), lambda i: (i, 0)),
        out_shape=jax.ShapeDtypeStruct((T, K), jnp.int32),
        compiler_params=_params("arbitrary"),
        name="moe_positions",
    )(sel, topi, base)


def _scatter_kernel(pe_ref, cn_ref, dest_ref, hp_ref, xs_ref, zbuf, sem, zsem, *, top_k, bm):
    tile = hp_ref.shape[0]
    n_exp = pe_ref.shape[0]

    n_blocks = xs_ref.shape[0] // bm
    n_used = pe_ref[n_exp - 1] // bm

    def zero_block(m):
        return pltpu.make_async_copy(zbuf, xs_ref.at[pl.ds(pl.multiple_of(m * bm, bm), bm)], zsem)

    def zero_tail(e):
        return zero_block(pe_ref[e] // bm - 1)

    @pl.when(pl.program_id(0) == 0)
    def _():
        zbuf[...] = jnp.zeros_like(zbuf)

        def start(e, c):
            @pl.when(cn_ref[e] > 0)
            def _():
                zero_tail(e).start()
            return c

        def wait(e, c):
            @pl.when(cn_ref[e] > 0)
            def _():
                zero_tail(e).wait()
            return c

        def start_unused(m, c):
            zero_block(m).start()
            return c

        def wait_unused(m, c):
            zero_block(m).wait()
            return c

        lax.fori_loop(0, n_exp, start, 0)
        lax.fori_loop(n_used, n_blocks, start_unused, 0)
        lax.fori_loop(0, n_exp, wait, 0)
        lax.fori_loop(n_used, n_blocks, wait_unused, 0)

    def row_copy(r, d):
        return pltpu.make_async_copy(hp_ref.at[pl.ds(r, 1)], xs_ref.at[pl.ds(d, 1)], sem)

    def issue(r, c):
        for k in range(top_k):
            row_copy(r, dest_ref[r * top_k + k]).start()
        return c

    def drain(r, c):
        for k in range(top_k):
            row_copy(r, dest_ref[r * top_k + k]).wait()
        return c

    lax.fori_loop(0, tile, issue, 0)
    lax.fori_loop(0, tile, drain, 0)


def _scatter_rows(hp, dest_flat, pad_end, counts, n_pad, tile, bm):
    T, W = hp.shape
    K = dest_flat.shape[0] // T
    grid_spec = pltpu.PrefetchScalarGridSpec(
        num_scalar_prefetch=2,
        grid=(T // tile,),
        in_specs=[
            pl.BlockSpec((tile * K,), lambda i, pe, cn: (i,), memory_space=pltpu.SMEM),
            pl.BlockSpec((tile, W), lambda i, pe, cn: (i, 0)),
        ],
        out_specs=pl.BlockSpec(memory_space=pl.ANY),
        scratch_shapes=[pltpu.VMEM((bm, W), jnp.uint32), pltpu.SemaphoreType.DMA, pltpu.SemaphoreType.DMA],
    )
    return pl.pallas_call(
        functools.partial(_scatter_kernel, top_k=K, bm=bm),
        grid_spec=grid_spec,
        out_shape=jax.ShapeDtypeStruct((n_pad, W), jnp.uint32),
        compiler_params=_params("arbitrary"),
        name="moe_scatter",
    )(pad_end, counts, dest_flat, hp)


def _expert_step(be_ref, nu_ref):
    m = pl.program_id(1)
    last = nu_ref[0] - 1
    mm = jnp.minimum(m, last)
    prev = jnp.maximum(mm - 1, 0)
    fresh = jnp.logical_or(m == 0, be_ref[mm] != be_ref[prev])
    return m <= last, fresh


def _moe_gu_kernel(be_ref, nu_ref, xs_ref, wg_ref, wu_ref, bg_ref, bu_ref, o_ref, wg_bf, wu_bf):
    live, fresh = _expert_step(be_ref, nu_ref)

    @pl.when(jnp.logical_and(live, fresh))
    def _():
        wg_bf[...] = wg_ref[...].astype(bf16)
        wu_bf[...] = wu_ref[...].astype(bf16)

    @pl.when(live)
    def _():
        lo, hi = _unpack_bf16_pairs(xs_ref[...])
        half = lo.shape[1]
        gate = _dot(lo, wg_bf[0:half, :]) + _dot(hi, wg_bf[half:, :]) + bg_ref[...]
        up = _dot(lo, wu_bf[0:half, :]) + _dot(hi, wu_bf[half:, :]) + bu_ref[...]
        gate = jnp.minimum(gate, SWIGLU_LIMIT)
        up = jnp.clip(up, -SWIGLU_LIMIT, SWIGLU_LIMIT)
        act = (up + 1.0) * (gate * jax.nn.sigmoid(SWIGLU_ALPHA * gate))
        o_ref[...] = act.astype(o_ref.dtype)

    @pl.when(jnp.logical_not(live))
    def _():
        o_ref[...] = jnp.zeros_like(o_ref)


def _moe_gate_up(xs, blk_e, n_used, gu_w, gu_b, bm):
    n_pad, W = xs.shape
    E, D, F2 = gu_w.shape
    F = F2 // 2
    tn = _pick(F, MOE_GU_COL_TILE, LANES)
    nb = F // tn
    row = lambda n, m, be, nu: jnp.minimum(m, nu[0] - 1)
    exp = lambda n, m, be, nu: be[jnp.minimum(m, nu[0] - 1)]
    grid_spec = pltpu.PrefetchScalarGridSpec(
        num_scalar_prefetch=2,
        grid=(nb, n_pad // bm),
        in_specs=[
            pl.BlockSpec((bm, W), lambda n, m, be, nu: (row(n, m, be, nu), 0)),
            pl.BlockSpec((None, D, tn), lambda n, m, be, nu: (exp(n, m, be, nu), 0, n)),
            pl.BlockSpec((None, D, tn), lambda n, m, be, nu: (exp(n, m, be, nu), 0, n + nb)),
            pl.BlockSpec((None, 1, tn), lambda n, m, be, nu: (exp(n, m, be, nu), 0, n)),
            pl.BlockSpec((None, 1, tn), lambda n, m, be, nu: (exp(n, m, be, nu), 0, n + nb)),
        ],
        out_specs=pl.BlockSpec((bm, tn), lambda n, m, be, nu: (m, n)),
        scratch_shapes=[pltpu.VMEM((D, tn), bf16), pltpu.VMEM((D, tn), bf16)],
    )
    gu_b3 = gu_b.reshape(E, 1, F2)
    return pl.pallas_call(
        _moe_gu_kernel,
        grid_spec=grid_spec,
        out_shape=jax.ShapeDtypeStruct((n_pad, F), bf16),
        compiler_params=_params("arbitrary", "arbitrary"),
        name="moe_gate_up",
    )(blk_e, n_used, xs, gu_w, gu_w, gu_b3, gu_b3)


def _moe_dn_kernel(be_ref, nu_ref, a_ref, w_ref, b_ref, o_ref, w_bf):
    live, fresh = _expert_step(be_ref, nu_ref)

    @pl.when(jnp.logical_and(live, fresh))
    def _():
        w_bf[...] = w_ref[...].astype(bf16)

    @pl.when(live)
    def _():
        o_ref[...] = _dot(a_ref[...], w_bf[...]) + b_ref[...]

    @pl.when(jnp.logical_not(live))
    def _():
        o_ref[...] = jnp.zeros_like(o_ref)


def _moe_down(act, blk_e, n_used, dn_w, dn_b, bm):
    n_pad, F = act.shape
    E, _, D = dn_w.shape
    tn = _pick(D, MOE_DN_COL_TILE, LANES)
    row = lambda n, m, be, nu: jnp.minimum(m, nu[0] - 1)
    exp = lambda n, m, be, nu: be[jnp.minimum(m, nu[0] - 1)]
    grid_spec = pltpu.PrefetchScalarGridSpec(
        num_scalar_prefetch=2,
        grid=(D // tn, n_pad // bm),
        in_specs=[
            pl.BlockSpec((bm, F), lambda n, m, be, nu: (row(n, m, be, nu), 0)),
            pl.BlockSpec((None, F, tn), lambda n, m, be, nu: (exp(n, m, be, nu), 0, n)),
            pl.BlockSpec((None, 1, tn), lambda n, m, be, nu: (exp(n, m, be, nu), 0, n)),
        ],
        out_specs=pl.BlockSpec((bm, tn), lambda n, m, be, nu: (m, n)),
        scratch_shapes=[pltpu.VMEM((F, tn), bf16)],
    )
    return pl.pallas_call(
        _moe_dn_kernel,
        grid_spec=grid_spec,
        out_shape=jax.ShapeDtypeStruct((n_pad, D), f32),
        compiler_params=_params("arbitrary", "arbitrary"),
        name="moe_down",
    )(blk_e, n_used, act, dn_w, dn_b.reshape(E, 1, D))


def _combine_kernel(dcur_ref, dnxt_ref, w_ref, x_ref, g_ref, ys_ref, o_ref, buf, sem, *, rg, n_tiles):
    tt, d = x_ref.shape
    top_k = w_ref.shape[1]
    i = pl.program_id(0)

    def row_copy(src_row, slot, k, r):
        return pltpu.make_async_copy(ys_ref.at[pl.ds(src_row, 1)], buf.at[slot, k, pl.ds(r, 1)], sem.at[slot])

    def issue(d_ref, slot):
        def body(r, c):
            for k in range(top_k):
                row_copy(d_ref[r * top_k + k], slot, k, r).start()
            return c
        lax.fori_loop(0, tt, body, 0)

    @pl.when(i == 0)
    def _():
        issue(dcur_ref, 0)

    @pl.when(i + 1 < n_tiles)
    def _():
        issue(dnxt_ref, (i + 1) % 2)

    slot = i % 2

    def drain(r, c):
        for k in range(top_k):
            row_copy(0, slot, k, r).wait()
        return c

    lax.fori_loop(0, tt, drain, 0)
    w = w_ref[...]
    f = w[:, 0:1] * buf[slot, 0]
    for k in range(1, top_k):
        f = f + w[:, k:k + 1] * buf[slot, k]
    upd = (f.reshape(tt // rg, rg, d) * g_ref[...]).reshape(tt, d)
    o_ref[...] = x_ref[...] + upd


def _combine(ys, dest_flat, topw, x, mod_g, gate_col, rg):
    T, D = x.shape
    K = topw.shape[1]
    tt = _pick(T, COMBINE_ROWS, rg)
    n_tiles = T // tt
    return pl.pallas_call(
        functools.partial(_combine_kernel, rg=rg, n_tiles=n_tiles),
        grid=(n_tiles,),
        in_specs=[
            pl.BlockSpec((tt * K,), lambda i: (i,), memory_space=pltpu.SMEM),
            pl.BlockSpec((tt * K,), lambda i: (jnp.minimum(i + 1, n_tiles - 1),), memory_space=pltpu.SMEM),
            pl.BlockSpec((tt, K), lambda i: (i, 0)),
            pl.BlockSpec((tt, D), lambda i: (i, 0)),
            pl.BlockSpec((tt // rg, 1, D), lambda i: (i, 0, gate_col)),
            pl.BlockSpec(memory_space=pl.ANY),
        ],
        out_specs=pl.BlockSpec((tt, D), lambda i: (i, 0)),
        out_shape=jax.ShapeDtypeStruct((T, D), f32),
        scratch_shapes=[pltpu.VMEM((2, K, tt, D), f32), pltpu.SemaphoreType.DMA((2,))],
        compiler_params=_params("arbitrary"),
        name="moe_combine",
    )(dest_flat, dest_flat, topw, x, mod_g, ys)


def _moe_layer(x, nw, mod_g, rw, rb, gu_w, gu_b, dn_w, dn_b, rg, tile):
    T, D = x.shape
    E = rw.shape[1]
    bm = MOE_BLOCK
    hp, topi, topw, sel, cnt = _norm_route(x, nw, mod_g, 3, 4, rw, rb, rg, tile)
    cnt_i = cnt[:, 0, :].astype(jnp.int32)
    counts = jnp.sum(cnt_i, axis=0)
    padded = (counts + bm - 1) // bm * bm
    pad_end = jnp.cumsum(padded)
    pad_start = pad_end - padded
    base = (pad_start[None, :] + jnp.cumsum(cnt_i, axis=0) - cnt_i).astype(f32)[:, None, :]
    n_blocks = -(-(T * TOP_K + E * (bm - 1)) // bm)
    blk_start = jnp.arange(n_blocks, dtype=jnp.int32) * bm
    blk_e = jnp.minimum(jnp.searchsorted(pad_end, blk_start, side="right"), E - 1).astype(jnp.int32)
    n_used = (pad_end[-1:] // bm).astype(jnp.int32)
    dest = _positions(sel, topi, base, tile).reshape(T * TOP_K)
    xs = _scatter_rows(hp, dest, pad_end.astype(jnp.int32), counts, n_blocks * bm, tile, bm)
    act = _moe_gate_up(xs, blk_e, n_used, gu_w, gu_b, bm)
    ys = _moe_down(act, blk_e, n_used, dn_w, dn_b, bm)
    return _combine(ys, dest, topw, x, mod_g, 5, rg)


def kernel(x_prompt, x_sample, c_prompt, c_sample, state_conv, cache_k, cache_v, ada_w, ada_b, norm_mix, norm_ffn, norm_final, conv_pw1_w, conv_pw1_b, conv_dw_w, conv_dw_b, conv_ln_g, conv_ln_b, conv_pw2_w, conv_pw2_b, attn_qkv_w, attn_qkv_b, attn_o_w, attn_o_b, attn_sinks, router_w, router_b, moe_gu_w, moe_gu_b, moe_dn_w, moe_dn_b):
    B, S, D = x_prompt.shape
    Bs, Ts, _ = x_sample.shape
    depth = ada_w.shape[0]
    n_kv, hd = cache_k.shape[3], cache_k.shape[4]
    n_heads = attn_sinks.shape[1]
    group = n_heads // n_kv
    w_cache = cache_k.shape[2]
    Tp, Tsm = B * S, Bs * Ts
    T = Tp + Tsm
    rg = Ts
    assert S % rg == 0 and rg % 8 == 0 and rg == CONV_ROWS and n_kv % 2 == 0 and 2 * hd == LANES
    tile = _pick(Tsm, ROW_TILE, rg)
    assert Tp % tile == 0

    x = jnp.concatenate([x_prompt.reshape(Tp, D), x_sample.reshape(Tsm, D)], axis=0)
    n_streams = B + Bs
    c_all = jnp.concatenate([c_prompt, c_sample], axis=0)
    c_pad = jnp.pad(c_all, ((0, -n_streams % STREAM_ROWS_PAD), (0, 0)))
    mod = _ada(c_pad, ada_w, ada_b)
    stream_of_group = jnp.concatenate([
        jnp.repeat(jnp.arange(B, dtype=jnp.int32), S // rg),
        B + jnp.repeat(jnp.arange(Bs, dtype=jnp.int32), Ts // rg)])
    conv_states, k_states, v_states = [], [], []
    for i in range(depth):
        mod_g = mod[i][stream_of_group][:, None, :]
        j = i // 2
        h = _norm_mod(x, norm_mix[i], mod_g, 0, 1, rg, tile)
        if i % 2 == 0:
            g = _glu_matmul(h, conv_pw1_w[j].astype(bf16), conv_pw1_b[j])
            hist = jnp.pad(state_conv[j], ((0, 0), (rg - (CONV_WIDTH - 1), 0), (0, 0))).reshape(Bs * rg, D)
            a = _conv_module(g, hist, conv_dw_w[j], conv_dw_b[j], conv_ln_g[j], conv_ln_b[j], Tp, S)
            x = _residual_matmul(a[None], conv_pw2_w[j].astype(bf16)[None], conv_pw2_b[j], x, mod_g, 2, rg,
                                 "pw2_residual")
            keep = CONV_WIDTH - 1
            conv_states.append((g[:Tp].reshape(B, S, D)[:, S - keep:], g[Tp:].reshape(Bs, Ts, D)[:, Ts - keep:]))
        else:
            qd = n_heads * hd
            kd = n_kv * hd
            wq = attn_qkv_w[j][:, :qd].reshape(D, n_kv, group, hd).transpose(0, 2, 1, 3).reshape(D, qd)
            bq = attn_qkv_b[j][:qd].reshape(n_kv, group, hd).transpose(1, 0, 2).reshape(qd)
            q_r = _linear_grouped_out(h, wq.astype(bf16), bq, group, bf16, "q_proj")
            kv = _linear(h, attn_qkv_w[j][:, qd:].astype(bf16), attn_qkv_b[j][qd:], f32, "kv_proj")
            o_r = _attn_prompt(q_r, kv, attn_sinks[j], n_kv, hd, S, Tp, WINDOW, CHUNK)
            kv_new = kv[Tp:].reshape(Bs, Ts, 2 * kd)
            cache = jnp.concatenate([cache_k[j].reshape(Bs, w_cache, kd), cache_v[j].reshape(Bs, w_cache, kd)], axis=-1)
            kv_all = jnp.concatenate([cache, kv_new], axis=1)
            o_r = _attn_sample(q_r, kv_all.reshape(Bs * (w_cache + Ts), 2 * kd), attn_sinks[j], o_r,
                               n_kv, hd, Ts, Tp, Bs)
            wo = attn_o_w[j].reshape(n_kv, group, hd, D).transpose(1, 0, 2, 3).reshape(group, kd, D)
            x = _residual_matmul(o_r, wo.astype(bf16), attn_o_b[j], x, mod_g, 2, rg, "o_residual")
            kvp = kv[:Tp].reshape(B, S, 2, n_kv, hd)[:, S - WINDOW:]
            kvs = kv_all[:, Ts:].reshape(Bs, w_cache, 2, n_kv, hd)
            k_states.append((kvp[:, :, 0], kvs[:, :, 0]))
            v_states.append((kvp[:, :, 1], kvs[:, :, 1]))
        x = _moe_layer(x, norm_ffn[i], mod_g, router_w[i], router_b[i], moe_gu_w[i], moe_gu_b[i],
                       moe_dn_w[i], moe_dn_b[i], rg, tile)
    y = _final_norm(x, norm_final, tile)
    stack = lambda states, idx: jnp.stack([s[idx] for s in states])
    return (y[:Tp].reshape(B, S, D), y[Tp:].reshape(Bs, Ts, D),
            stack(conv_states, 0), stack(conv_states, 1),
            stack(k_states, 0), stack(v_states, 0), stack(k_states, 1), stack(v_states, 1))
```

```python
import functools

import jax
import jax.numpy as jnp
from jax import lax
from jax.experimental import pallas as pl
from jax.experimental.pallas import tpu as pltpu

CHUNK = 64
WINDOW = 128
CONV_WIDTH = 31
TOP_K = 4
SWIGLU_LIMIT = 7.0
SWIGLU_ALPHA = 1.702
EPS = 1e-5
NEG_INF = -1e30

LANES = 128
SUBLANES = 8
STREAM_ROWS_PAD = 16
VMEM_LIMIT_BYTES = 56 * 1024 * 1024
ROW_TILE = 256
MM_ROW_CAP = 1536
MM_COL_TILE = 512
MOE_BLOCK = 512
MOE_GU_COL_TILE = 512
MOE_DN_COL_TILE = 1024
CONV_ROWS = 32
CONV_COL_CHUNK = 512
ATTN_ROWS = 256
COMBINE_ROWS = 64
ADA_COL_TILE = 768

f32 = jnp.float32
bf16 = jnp.bfloat16


def _pick(total, cap, mult):
    best = None
    for t in range(mult, min(total, cap) + 1, mult):
        if total % t == 0:
            best = t
    assert best is not None, (total, cap, mult)
    return best


def _params(*sem):
    return pltpu.CompilerParams(dimension_semantics=sem, vmem_limit_bytes=VMEM_LIMIT_BYTES)


def _dot(a, b):
    return jnp.dot(a, b, preferred_element_type=f32)


def _ada_kernel(c_ref, w_ref, b_ref, o_ref):
    c = c_ref[...]
    s = (c * jax.nn.sigmoid(c)).astype(bf16)
    o_ref[...] = _dot(s, w_ref[...].astype(bf16)) + b_ref[...]


def _ada(c_pad, ada_w, ada_b):
    L, D, N = ada_w.shape
    R = c_pad.shape[0]
    tn = _pick(N, ADA_COL_TILE, LANES)
    return pl.pallas_call(
        _ada_kernel,
        grid=(L, N // tn),
        in_specs=[
            pl.BlockSpec((R, D), lambda l, n: (0, 0)),
            pl.BlockSpec((None, D, tn), lambda l, n: (l, 0, n)),
            pl.BlockSpec((None, 1, tn), lambda l, n: (l, 0, n)),
        ],
        out_specs=pl.BlockSpec((None, R, tn), lambda l, n: (l, 0, n)),
        out_shape=jax.ShapeDtypeStruct((L, R, N), f32),
        compiler_params=_params("arbitrary", "arbitrary"),
        name="ada",
    )(c_pad, ada_w, ada_b.reshape(L, 1, N))


def _modulated_norm(x, nw, scale, shift, rg):
    t, d = x.shape
    y = x * lax.rsqrt(jnp.mean(x * x, axis=-1, keepdims=True) + EPS) * nw
    h = y.reshape(t // rg, rg, d) * (1.0 + scale) + shift
    return h.reshape(t, d)


def _norm_kernel(x_ref, nw_ref, sc_ref, sh_ref, h_ref, *, rg):
    h = _modulated_norm(x_ref[...], nw_ref[...], sc_ref[...], sh_ref[...], rg)
    h_ref[...] = h.astype(h_ref.dtype)


def _norm_mod(x, nw, mod_g, shift_col, scale_col, rg, tile):
    T, D = x.shape
    gt = tile // rg
    return pl.pallas_call(
        functools.partial(_norm_kernel, rg=rg),
        grid=(T // tile,),
        in_specs=[
            pl.BlockSpec((tile, D), lambda i: (i, 0)),
            pl.BlockSpec((1, D), lambda i: (0, 0)),
            pl.BlockSpec((gt, 1, D), lambda i: (i, 0, scale_col)),
            pl.BlockSpec((gt, 1, D), lambda i: (i, 0, shift_col)),
        ],
        out_specs=pl.BlockSpec((tile, D), lambda i: (i, 0)),
        out_shape=jax.ShapeDtypeStruct((T, D), bf16),
        compiler_params=_params("arbitrary"),
        name="norm_mod",
    )(x, nw.reshape(1, D), mod_g, mod_g)


def _pack_bf16_pairs(h):
    d = h.shape[1]
    bits = pltpu.bitcast(h.astype(bf16).astype(f32), jnp.uint32)
    return (bits[:, : d // 2] >> 16) | (bits[:, d // 2:] & jnp.uint32(0xFFFF0000))


def _unpack_bf16_pairs(w):
    lo = pltpu.bitcast(w << 16, f32).astype(bf16)
    hi = pltpu.bitcast(w & jnp.uint32(0xFFFF0000), f32).astype(bf16)
    return lo, hi


def _norm_route_kernel(x_ref, nw_ref, sc_ref, sh_ref, rw_ref, rb_ref,
                       hp_ref, topi_ref, topw_ref, sel_ref, cnt_ref, *, rg, top_k):
    h = _modulated_norm(x_ref[...], nw_ref[...], sc_ref[...], sh_ref[...], rg)
    hp_ref[...] = _pack_bf16_pairs(h)
    logits = _dot(h.astype(bf16), rw_ref[...]) + rb_ref[...]
    t, e = logits.shape
    lane = lax.broadcasted_iota(jnp.int32, (t, e), 1).astype(f32)
    kcol = lax.broadcasted_iota(jnp.int32, (t, top_k), 1)
    sel = jnp.zeros((t, e), f32)
    topi = jnp.zeros((t, top_k), f32)
    topv = jnp.zeros((t, top_k), f32)
    work = logits
    for k in range(top_k):
        m = jnp.max(work, axis=-1, keepdims=True)
        idx = jnp.min(jnp.where(work == m, lane, float(e)), axis=-1, keepdims=True)
        hit = lane == idx
        sel = sel + hit.astype(f32)
        topi = jnp.where(kcol == k, idx, topi)
        topv = jnp.where(kcol == k, m, topv)
        work = jnp.where(hit, -jnp.inf, work)
    p = jnp.exp(topv - topv[:, 0:1])
    topw_ref[...] = p / jnp.sum(p, axis=-1, keepdims=True)
    topi_ref[...] = topi.astype(jnp.int32)
    sel_ref[...] = sel.astype(sel_ref.dtype)
    cnt_ref[...] = jnp.sum(sel, axis=0, keepdims=True)


def _norm_route(x, nw, mod_g, shift_col, scale_col, rw, rb, rg, tile):
    T, D = x.shape
    E = rw.shape[1]
    gt = tile // rg
    nt = T // tile
    return pl.pallas_call(
        functools.partial(_norm_route_kernel, rg=rg, top_k=TOP_K),
        grid=(nt,),
        in_specs=[
            pl.BlockSpec((tile, D), lambda i: (i, 0)),
            pl.BlockSpec((1, D), lambda i: (0, 0)),
            pl.BlockSpec((gt, 1, D), lambda i: (i, 0, scale_col)),
            pl.BlockSpec((gt, 1, D), lambda i: (i, 0, shift_col)),
            pl.BlockSpec((D, E), lambda i: (0, 0)),
            pl.BlockSpec((1, E), lambda i: (0, 0)),
        ],
        out_specs=[
            pl.BlockSpec((tile, D // 2), lambda i: (i, 0)),
            pl.BlockSpec((tile, TOP_K), lambda i: (i, 0)),
            pl.BlockSpec((tile, TOP_K), lambda i: (i, 0)),
            pl.BlockSpec((tile, E), lambda i: (i, 0)),
            pl.BlockSpec((None, 1, E), lambda i: (i, 0, 0)),
        ],
        out_shape=[
            jax.ShapeDtypeStruct((T, D // 2), jnp.uint32),
            jax.ShapeDtypeStruct((T, TOP_K), jnp.int32),
            jax.ShapeDtypeStruct((T, TOP_K), f32),
            jax.ShapeDtypeStruct((T, E), bf16),
            jax.ShapeDtypeStruct((nt, 1, E), f32),
        ],
        compiler_params=_params("arbitrary"),
        name="norm_route",
    )(x, nw.reshape(1, D), mod_g, mod_g, rw.astype(bf16), rb.reshape(1, E))


def _final_norm_kernel(x_ref, nw_ref, o_ref):
    x = x_ref[...]
    o_ref[...] = x * lax.rsqrt(jnp.mean(x * x, axis=-1, keepdims=True) + EPS) * nw_ref[...]


def _final_norm(x, nw, tile, row0, n_rows):
    D = x.shape[1]
    blk0 = row0 // tile
    return pl.pallas_call(
        _final_norm_kernel,
        grid=(n_rows // tile,),
        in_specs=[pl.BlockSpec((tile, D), lambda i: (blk0 + i, 0)), pl.BlockSpec((1, D), lambda i: (0, 0))],
        out_specs=pl.BlockSpec((tile, D), lambda i: (i, 0)),
        out_shape=jax.ShapeDtypeStruct((n_rows, D), f32),
        compiler_params=_params("arbitrary"),
        name="final_norm",
    )(x, nw.reshape(1, D))


def _glu_kernel(x_ref, wa_ref, wb_ref, ba_ref, bb_ref, o_ref):
    x = x_ref[...]
    a = _dot(x, wa_ref[...]) + ba_ref[...]
    b = _dot(x, wb_ref[...]) + bb_ref[...]
    o_ref[...] = a * jax.nn.sigmoid(b)


def _glu_matmul(x, w, b):
    T, K = x.shape
    N = w.shape[1] // 2
    tm = _pick(T, MM_ROW_CAP, 32)
    tn = _pick(N, MM_COL_TILE, LANES)
    nb = N // tn
    b2 = b.reshape(1, 2 * N)
    return pl.pallas_call(
        _glu_kernel,
        grid=(T // tm, nb),
        in_specs=[
            pl.BlockSpec((tm, K), lambda m, n: (m, 0)),
            pl.BlockSpec((K, tn), lambda m, n: (0, n)),
            pl.BlockSpec((K, tn), lambda m, n: (0, n + nb)),
            pl.BlockSpec((1, tn), lambda m, n: (0, n)),
            pl.BlockSpec((1, tn), lambda m, n: (0, n + nb)),
        ],
        out_specs=pl.BlockSpec((tm, tn), lambda m, n: (m, n)),
        out_shape=jax.ShapeDtypeStruct((T, N), f32),
        compiler_params=_params("arbitrary", "arbitrary"),
        name="pw1_glu",
    )(x, w, w, b2, b2)


def _linear_kernel(x_ref, w_ref, b_ref, o_ref):
    o_ref[...] = (_dot(x_ref[...], w_ref[...]) + b_ref[...]).astype(o_ref.dtype)


def _linear_grouped_out(x, w, b, groups, out_dtype, name):
    T, K = x.shape
    ng = w.shape[1] // groups
    tm = _pick(T, MM_ROW_CAP, 32)
    return pl.pallas_call(
        _linear_kernel,
        grid=(T // tm, groups),
        in_specs=[
            pl.BlockSpec((tm, K), lambda m, n: (m, 0)),
            pl.BlockSpec((K, ng), lambda m, n: (0, n)),
            pl.BlockSpec((1, ng), lambda m, n: (0, n)),
        ],
        out_specs=pl.BlockSpec((None, tm, ng), lambda m, n: (n, m, 0)),
        out_shape=jax.ShapeDtypeStruct((groups, T, ng), out_dtype),
        compiler_params=_params("arbitrary", "arbitrary"),
        name=name,
    )(x, w, b.reshape(1, -1))


def _linear(x, w, b, out_dtype, name):
    T, K = x.shape
    N = w.shape[1]
    tm = _pick(T, MM_ROW_CAP, 32)
    tn = _pick(N, MM_COL_TILE, LANES)
    return pl.pallas_call(
        _linear_kernel,
        grid=(T // tm, N // tn),
        in_specs=[
            pl.BlockSpec((tm, K), lambda m, n: (m, 0)),
            pl.BlockSpec((K, tn), lambda m, n: (0, n)),
            pl.BlockSpec((1, tn), lambda m, n: (0, n)),
        ],
        out_specs=pl.BlockSpec((tm, tn), lambda m, n: (m, n)),
        out_shape=jax.ShapeDtypeStruct((T, N), out_dtype),
        compiler_params=_params("arbitrary", "arbitrary"),
        name=name,
    )(x, w, b.reshape(1, -1))


def _residual_kernel(l_ref, w_ref, b_ref, xr_ref, g_ref, o_ref, *, rg):
    groups = l_ref.shape[0]
    acc = _dot(l_ref[0], w_ref[0])
    for g in range(1, groups):
        acc = acc + _dot(l_ref[g], w_ref[g])
    y = acc + b_ref[...]
    tm, tn = y.shape
    upd = (y.reshape(tm // rg, rg, tn) * g_ref[...]).reshape(tm, tn)
    o_ref[...] = xr_ref[...] + upd


def _residual_matmul(lhs, w, b, x_res, mod_g, gate_col, rg, name):
    G, T, Kg = lhs.shape
    N = w.shape[2]
    tm = _pick(T, MM_ROW_CAP, rg)
    tn = _pick(N, MM_COL_TILE, LANES)
    nb = N // tn
    return pl.pallas_call(
        functools.partial(_residual_kernel, rg=rg),
        grid=(T // tm, nb),
        in_specs=[
            pl.BlockSpec((G, tm, Kg), lambda m, n: (0, m, 0)),
            pl.BlockSpec((G, Kg, tn), lambda m, n: (0, 0, n)),
            pl.BlockSpec((1, tn), lambda m, n: (0, n)),
            pl.BlockSpec((tm, tn), lambda m, n: (m, n)),
            pl.BlockSpec((tm // rg, 1, tn), lambda m, n: (m, 0, gate_col * nb + n)),
        ],
        out_specs=pl.BlockSpec((tm, tn), lambda m, n: (m, n)),
        out_shape=jax.ShapeDtypeStruct((T, N), f32),
        compiler_params=_params("arbitrary", "arbitrary"),
        name=name,
    )(lhs, w, b.reshape(1, N), x_res, mod_g)


def _conv_kernel(g_ref, prev_ref, hist_ref, w_ref, b_ref, lg_ref, lb_ref, o_ref, phase, ybuf, *,
                 prompt_steps, steps_per_seq, col_chunk):
    tt, d = g_ref.shape
    hb = prev_ref.shape[0]
    taps = w_ref.shape[0]
    i = pl.program_id(0)
    prev = jnp.where((i % steps_per_seq) == 0, 0.0, prev_ref[...])
    phase[0, 0:hb, :] = jnp.where(i >= prompt_steps, hist_ref[...], prev)
    phase[0, hb:hb + tt, :] = g_ref[...]
    lo, hi = hb - (taps - 1), hb
    steps = {b: [a for a in range(hi // SUBLANES + 1) if lo <= SUBLANES * a + b <= hi] for b in range(SUBLANES)}
    for b in range(1, SUBLANES):
        if steps[b]:
            rows = SUBLANES * steps[b][-1] + tt
            phase[b, 0:rows, :] = phase[0, b:b + rows, :]
    for c in range(d // col_chunk):
        cs = slice(c * col_chunk, (c + 1) * col_chunk)
        acc = jnp.broadcast_to(b_ref[:, cs], (tt, col_chunk)).reshape(tt // SUBLANES, SUBLANES, col_chunk)
        for b in range(SUBLANES):
            if not steps[b]:
                continue
            a0 = steps[b][0]
            span = phase[b, SUBLANES * a0:SUBLANES * steps[b][-1] + tt, cs]
            for a in steps[b]:
                k = SUBLANES * a + b - lo
                rows = span[SUBLANES * (a - a0):SUBLANES * (a - a0) + tt, :]
                acc = acc + rows.reshape(tt // SUBLANES, SUBLANES, col_chunk) * w_ref[k, :, cs]
        ybuf[:, cs] = acc.reshape(tt, col_chunk)
    y = ybuf[...]
    mu = jnp.mean(y, axis=-1, keepdims=True)
    yc = y - mu
    var = jnp.mean(yc * yc, axis=-1, keepdims=True)
    yn = yc * lax.rsqrt(var + EPS) * lg_ref[...] + lb_ref[...]
    o_ref[...] = (yn * jax.nn.sigmoid(yn)).astype(o_ref.dtype)


def _conv_module(g, hist, dw_w, dw_b, ln_g, ln_b, n_prompt_rows, seq):
    T, D = g.shape
    tt = CONV_ROWS
    taps = dw_w.shape[0]
    prompt_steps = n_prompt_rows // tt
    last_hist = hist.shape[0] // tt - 1
    kern = functools.partial(_conv_kernel, prompt_steps=prompt_steps, steps_per_seq=seq // tt,
                             col_chunk=_pick(D, CONV_COL_CHUNK, LANES))
    vec = pl.BlockSpec((1, D), lambda i: (0, 0))
    return pl.pallas_call(
        kern,
        grid=(T // tt,),
        in_specs=[
            pl.BlockSpec((tt, D), lambda i: (i, 0)),
            pl.BlockSpec((tt, D), lambda i: (jnp.maximum(i - 1, 0), 0)),
            pl.BlockSpec((tt, D), lambda i: (jnp.clip(i - prompt_steps, 0, last_hist), 0)),
            pl.BlockSpec((taps, SUBLANES, D), lambda i: (0, 0, 0)),
            vec, vec, vec,
        ],
        out_specs=pl.BlockSpec((tt, D), lambda i: (i, 0)),
        out_shape=jax.ShapeDtypeStruct((T, D), bf16),
        scratch_shapes=[pltpu.VMEM((SUBLANES, 2 * tt, D), f32), pltpu.VMEM((tt, D), f32)],
        compiler_params=_params("arbitrary"),
        name="conv_ln_swish",
    )(g, g, hist, jnp.broadcast_to(dw_w[:, None, :], (taps, SUBLANES, D)),
      dw_b.reshape(1, D), ln_g.reshape(1, D), ln_b.reshape(1, D))


def _attn_prompt_kernel(*refs, live_tiles, **kw):
    o_ref = refs[7]
    live = pl.program_id(1) < live_tiles

    @pl.when(live)
    def _():
        _attn_kernel(*refs, **kw)

    @pl.when(jnp.logical_not(live))
    def _():
        o_ref[...] = jnp.zeros_like(o_ref)


def _attn_kernel(*refs, n_chunks, cq, has_halo, scale, chunks_per_seq):
    if has_halo:
        q_ref, km_ref, kh_ref, vm_ref, vh_ref, bias_ref, sink_ref, o_ref, kbuf, vbuf = refs
        hr = kh_ref.shape[0]
        kbuf[0:hr, :] = kh_ref[...]
        vbuf[0:hr, :] = vh_ref[...]
    else:
        q_ref, km_ref, vm_ref, bias_ref, sink_ref, o_ref, kbuf, vbuf = refs
        hr = 0
    mr = km_ref.shape[0]
    kbuf[hr:hr + mr, :] = km_ref[...]
    vbuf[hr:hr + mr, :] = vm_ref[...]
    G = q_ref.shape[0]
    pw = q_ref.shape[2]
    hd = pw // 2
    nb = bias_ref.shape[2]
    rows = G * cq
    lane_k = lax.broadcasted_iota(jnp.int32, (nb, pw), 1)
    lane_o = lax.broadcasted_iota(jnp.int32, (rows, pw), 1)
    col = lax.broadcasted_iota(jnp.int32, (rows, nb), 1)
    first_chunk = (pl.program_id(1) % (chunks_per_seq // n_chunks)) * n_chunks if has_halo else None
    for j in range(n_chunks):
        q = q_ref[:, j * cq:(j + 1) * cq, :].reshape(rows, pw)
        kp = kbuf[j * cq:j * cq + nb, :]
        vp = vbuf[j * cq:j * cq + nb, :].astype(bf16)
        outs = []
        for hh in range(2):
            in_head = (lane_k < hd) if hh == 0 else (lane_k >= hd)
            kh = jnp.where(in_head, kp, 0.0).astype(bf16)
            s = lax.dot_general(q, kh, (((1,), (1,)), ((), ())), preferred_element_type=f32)
            s = s * scale + bias_ref[hh]
            if has_halo:
                kpos0 = (first_chunk + j) * cq - (nb - cq)
                s = jnp.where(col + kpos0 >= 0, s, NEG_INF)
            sink = sink_ref[hh]
            m = jnp.maximum(jnp.max(s, axis=-1, keepdims=True), sink)
            p = jnp.exp(s - m)
            den = jnp.sum(p, axis=-1, keepdims=True) + jnp.exp(sink - m)
            o = _dot(p.astype(bf16), vp)
            outs.append(o / den)
        o_pair = jnp.where(lane_o < hd, outs[0], outs[1])
        o_ref[:, j * cq:(j + 1) * cq, :] = o_pair.reshape(G, cq, pw).astype(o_ref.dtype)


def _attn_tables(sinks, n_kv, group, cq, nb, window):
    n_heads = n_kv * group
    slopes = jnp.exp2(-8.0 * jnp.arange(1, n_heads + 1, dtype=f32) / n_heads).reshape(n_kv, group)
    qi = jnp.arange(cq, dtype=jnp.int32)[:, None]
    si = jnp.arange(nb, dtype=jnp.int32)[None, :]
    dist = jnp.abs(window + qi - si).astype(f32)
    bias = -slopes[:, :, None, None] * dist[None, None]
    bias = bias.reshape(n_kv, group * cq, nb)
    sink = jnp.broadcast_to(sinks.astype(f32).reshape(n_kv, group, 1, 1), (n_kv, group, cq, 1))
    return bias, sink.reshape(n_kv, group * cq, 1)


def _attn_prompt(q_r, kv, sinks, n_kv, hd, seq, n_rows, window, chunk):
    G, T, Q = q_r.shape
    pw = 2 * hd
    pairs = n_kv // 2
    tr = _pick(seq, ATTN_ROWS, max(chunk, window))
    n_chunks = tr // chunk
    nb = window + chunk
    bias, sink = _attn_tables(sinks, n_kv, G, chunk, nb, window)
    hpb = tr // window
    assert T % tr == 0 and n_rows % tr == 0
    kern = functools.partial(_attn_prompt_kernel, live_tiles=n_rows // tr, n_chunks=n_chunks, cq=chunk,
                             has_halo=True, scale=hd ** -0.5, chunks_per_seq=seq // chunk)
    halo = lambda off: (lambda p, i: (jnp.maximum(i * hpb - 1, 0), off + p))
    return pl.pallas_call(
        kern,
        grid=(pairs, T // tr),
        in_specs=[
            pl.BlockSpec((G, tr, pw), lambda p, i: (0, i, p)),
            pl.BlockSpec((tr, pw), lambda p, i: (i, p)),
            pl.BlockSpec((window, pw), halo(0)),
            pl.BlockSpec((tr, pw), lambda p, i: (i, pairs + p)),
            pl.BlockSpec((window, pw), halo(pairs)),
            pl.BlockSpec((2, G * chunk, nb), lambda p, i: (p, 0, 0)),
            pl.BlockSpec((2, G * chunk, 1), lambda p, i: (p, 0, 0)),
        ],
        out_specs=pl.BlockSpec((G, tr, pw), lambda p, i: (0, i, p)),
        out_shape=jax.ShapeDtypeStruct((G, T, Q), bf16),
        scratch_shapes=[pltpu.VMEM((window + tr, pw), f32), pltpu.VMEM((window + tr, pw), f32)],
        compiler_params=_params("arbitrary", "arbitrary"),
        name="attn_prompt",
    )(q_r, kv, kv, kv, kv, bias, sink)


def _attn_sample(q_r, kv_all, sinks, o_prev, n_kv, hd, t_new, row0, n_streams):
    G, T, Q = q_r.shape
    pw = 2 * hd
    pairs = n_kv // 2
    nb = kv_all.shape[0] // n_streams
    bias, sink = _attn_tables(sinks, n_kv, G, t_new, nb, nb - t_new)
    kern = functools.partial(_attn_kernel, n_chunks=1, cq=t_new, has_halo=False,
                             scale=hd ** -0.5, chunks_per_seq=1)
    blk0 = row0 // t_new
    return pl.pallas_call(
        lambda *refs: kern(*refs[:5], *refs[6:]),
        grid=(pairs, n_streams),
        in_specs=[
            pl.BlockSpec((G, t_new, pw), lambda p, i: (0, blk0 + i, p)),
            pl.BlockSpec((nb, pw), lambda p, i: (i, p)),
            pl.BlockSpec((nb, pw), lambda p, i: (i, pairs + p)),
            pl.BlockSpec((2, G * t_new, nb), lambda p, i: (p, 0, 0)),
            pl.BlockSpec((2, G * t_new, 1), lambda p, i: (p, 0, 0)),
            pl.BlockSpec(memory_space=pl.ANY),
        ],
        out_specs=pl.BlockSpec((G, t_new, pw), lambda p, i: (0, blk0 + i, p)),
        out_shape=jax.ShapeDtypeStruct((G, T, Q), bf16),
        scratch_shapes=[pltpu.VMEM((nb, pw), f32), pltpu.VMEM((nb, pw), f32)],
        input_output_aliases={5: 0},
        compiler_params=_params("arbitrary", "arbitrary"),
        name="attn_sample",
    )(q_r, kv_all, kv_all, bias, sink, o_prev)


def _pos_kernel(sel_ref, topi_ref, base_ref, dest_ref):
    t, e = sel_ref.shape
    top_k = topi_ref.shape[1]
    r = lax.broadcasted_iota(jnp.int32, (t, t), 0)
    c = lax.broadcasted_iota(jnp.int32, (t, t), 1)
    earlier = (r > c).astype(bf16)
    rank = _dot(earlier, sel_ref[...]) + base_ref[...]
    lane = lax.broadcasted_iota(jnp.int32, (t, e), 1)
    kcol = lax.broadcasted_iota(jnp.int32, (t, top_k), 1)
    topi = topi_ref[...]
    dest = jnp.zeros((t, top_k), f32)
    for k in range(top_k):
        dk = jnp.sum(jnp.where(lane == topi[:, k:k + 1], rank, 0.0), axis=-1, keepdims=True)
        dest = jnp.where(kcol == k, dk, dest)
    dest_ref[...] = dest.astype(jnp.int32)


def _positions(sel, topi, base, tile):
    T, E = sel.shape
    K = topi.shape[1]
    return pl.pallas_call(
        _pos_kernel,
        grid=(T // tile,),
        in_specs=[
            pl.BlockSpec((tile, E), lambda i: (i, 0)),
            pl.BlockSpec((tile, K), lambda i: (i, 0)),
            pl.BlockSpec((None, 1, E), lambda i: (i, 0, 0)),
        ],
        out_specs=pl.BlockSpec((tile, K), lambda i: (i, 0)),
        out_shape=jax.ShapeDtypeStruct((T, K), jnp.int32),
        compiler_params=_params("arbitrary"),
        name="moe_positions",
    )(sel, topi, base)


def _scatter_kernel(pe_ref, cn_ref, dest_ref, hp_ref, xs_ref, zbuf, sem, zsem, *, top_k, bm):
    tile = hp_ref.shape[0]
    n_exp = pe_ref.shape[0]

    n_blocks = xs_ref.shape[0] // bm
    n_used = pe_ref[n_exp - 1] // bm

    def zero_block(m):
        return pltpu.make_async_copy(zbuf, xs_ref.at[pl.ds(pl.multiple_of(m * bm, bm), bm)], zsem)

    def zero_tail(e):
        return zero_block(pe_ref[e] // bm - 1)

    @pl.when(pl.program_id(0) == 0)
    def _():
        zbuf[...] = jnp.zeros_like(zbuf)

        def start(e, c):
            @pl.when(cn_ref[e] > 0)
            def _():
                zero_tail(e).start()
            return c

        def wait(e, c):
            @pl.when(cn_ref[e] > 0)
            def _():
                zero_tail(e).wait()
            return c

        def start_unused(m, c):
            zero_block(m).start()
            return c

        def wait_unused(m, c):
            zero_block(m).wait()
            return c

        lax.fori_loop(0, n_exp, start, 0)
        lax.fori_loop(n_used, n_blocks, start_unused, 0)
        lax.fori_loop(0, n_exp, wait, 0)
        lax.fori_loop(n_used, n_blocks, wait_unused, 0)

    def row_copy(r, d):
        return pltpu.make_async_copy(hp_ref.at[pl.ds(r, 1)], xs_ref.at[pl.ds(d, 1)], sem)

    def issue(r, c):
        for k in range(top_k):
            row_copy(r, dest_ref[r * top_k + k]).start()
        return c

    def drain(r, c):
        for k in range(top_k):
            row_copy(r, dest_ref[r * top_k + k]).wait()
        return c

    lax.fori_loop(0, tile, issue, 0)
    lax.fori_loop(0, tile, drain, 0)


def _scatter_rows(hp, dest_flat, pad_end, counts, n_pad, tile, bm):
    T, W = hp.shape
    K = dest_flat.shape[0] // T
    grid_spec = pltpu.PrefetchScalarGridSpec(
        num_scalar_prefetch=2,
        grid=(T // tile,),
        in_specs=[
            pl.BlockSpec((tile * K,), lambda i, pe, cn: (i,), memory_space=pltpu.SMEM),
            pl.BlockSpec((tile, W), lambda i, pe, cn: (i, 0)),
        ],
        out_specs=pl.BlockSpec(memory_space=pl.ANY),
        scratch_shapes=[pltpu.VMEM((bm, W), jnp.uint32), pltpu.SemaphoreType.DMA, pltpu.SemaphoreType.DMA],
    )
    return pl.pallas_call(
        functools.partial(_scatter_kernel, top_k=K, bm=bm),
        grid_spec=grid_spec,
        out_shape=jax.ShapeDtypeStruct((n_pad, W), jnp.uint32),
        compiler_params=_params("arbitrary"),
        name="moe_scatter",
    )(pad_end, counts, dest_flat, hp)


def _expert_step(be_ref, nu_ref):
    m = pl.program_id(1)
    last = nu_ref[0] - 1
    mm = jnp.minimum(m, last)
    prev = jnp.maximum(mm - 1, 0)
    fresh = jnp.logical_or(m == 0, be_ref[mm] != be_ref[prev])
    return m <= last, fresh


def _moe_gu_kernel(be_ref, nu_ref, xs_ref, wg_ref, wu_ref, bg_ref, bu_ref, o_ref, wg_bf, wu_bf):
    live, fresh = _expert_step(be_ref, nu_ref)

    @pl.when(jnp.logical_and(live, fresh))
    def _():
        wg_bf[...] = wg_ref[...].astype(bf16)
        wu_bf[...] = wu_ref[...].astype(bf16)

    @pl.when(live)
    def _():
        lo, hi = _unpack_bf16_pairs(xs_ref[...])
        half = lo.shape[1]
        gate = _dot(lo, wg_bf[0:half, :]) + _dot(hi, wg_bf[half:, :]) + bg_ref[...]
        up = _dot(lo, wu_bf[0:half, :]) + _dot(hi, wu_bf[half:, :]) + bu_ref[...]
        gate = jnp.minimum(gate, SWIGLU_LIMIT)
        up = jnp.clip(up, -SWIGLU_LIMIT, SWIGLU_LIMIT)
        act = (up + 1.0) * (gate * jax.nn.sigmoid(SWIGLU_ALPHA * gate))
        o_ref[...] = act.astype(o_ref.dtype)

    @pl.when(jnp.logical_not(live))
    def _():
        o_ref[...] = jnp.zeros_like(o_ref)


def _moe_gate_up(xs, blk_e, n_used, gu_w, gu_b, layer, bm):
    n_pad, W = xs.shape
    L, E, D, F2 = gu_w.shape
    F = F2 // 2
    tn = _pick(F, MOE_GU_COL_TILE, LANES)
    nb = F // tn
    row = lambda n, m, be, nu: jnp.minimum(m, nu[0] - 1)
    exp = lambda n, m, be, nu: be[jnp.minimum(m, nu[0] - 1)]
    grid_spec = pltpu.PrefetchScalarGridSpec(
        num_scalar_prefetch=2,
        grid=(nb, n_pad // bm),
        in_specs=[
            pl.BlockSpec((bm, W), lambda n, m, be, nu: (row(n, m, be, nu), 0)),
            pl.BlockSpec((None, None, D, tn), lambda n, m, be, nu: (layer, exp(n, m, be, nu), 0, n)),
            pl.BlockSpec((None, None, D, tn), lambda n, m, be, nu: (layer, exp(n, m, be, nu), 0, n + nb)),
            pl.BlockSpec((None, None, 1, tn), lambda n, m, be, nu: (layer, exp(n, m, be, nu), 0, n)),
            pl.BlockSpec((None, None, 1, tn), lambda n, m, be, nu: (layer, exp(n, m, be, nu), 0, n + nb)),
        ],
        out_specs=pl.BlockSpec((bm, tn), lambda n, m, be, nu: (m, n)),
        scratch_shapes=[pltpu.VMEM((D, tn), bf16), pltpu.VMEM((D, tn), bf16)],
    )
    gu_b3 = gu_b.reshape(L, E, 1, F2)
    return pl.pallas_call(
        _moe_gu_kernel,
        grid_spec=grid_spec,
        out_shape=jax.ShapeDtypeStruct((n_pad, F), bf16),
        compiler_params=_params("arbitrary", "arbitrary"),
        name="moe_gate_up",
    )(blk_e, n_used, xs, gu_w, gu_w, gu_b3, gu_b3)


def _moe_dn_kernel(be_ref, nu_ref, a_ref, w_ref, b_ref, o_ref, w_bf):
    live, fresh = _expert_step(be_ref, nu_ref)

    @pl.when(jnp.logical_and(live, fresh))
    def _():
        w_bf[...] = w_ref[...].astype(bf16)

    @pl.when(live)
    def _():
        o_ref[...] = _dot(a_ref[...], w_bf[...]) + b_ref[...]

    @pl.when(jnp.logical_not(live))
    def _():
        o_ref[...] = jnp.zeros_like(o_ref)


def _moe_down(act, blk_e, n_used, dn_w, dn_b, layer, bm):
    n_pad, F = act.shape
    L, E, _, D = dn_w.shape
    tn = _pick(D, MOE_DN_COL_TILE, LANES)
    row = lambda n, m, be, nu: jnp.minimum(m, nu[0] - 1)
    exp = lambda n, m, be, nu: be[jnp.minimum(m, nu[0] - 1)]
    grid_spec = pltpu.PrefetchScalarGridSpec(
        num_scalar_prefetch=2,
        grid=(D // tn, n_pad // bm),
        in_specs=[
            pl.BlockSpec((bm, F), lambda n, m, be, nu: (row(n, m, be, nu), 0)),
            pl.BlockSpec((None, None, F, tn), lambda n, m, be, nu: (layer, exp(n, m, be, nu), 0, n)),
            pl.BlockSpec((None, None, 1, tn), lambda n, m, be, nu: (layer, exp(n, m, be, nu), 0, n)),
        ],
        out_specs=pl.BlockSpec((bm, tn), lambda n, m, be, nu: (m, n)),
        scratch_shapes=[pltpu.VMEM((F, tn), bf16)],
    )
    return pl.pallas_call(
        _moe_dn_kernel,
        grid_spec=grid_spec,
        out_shape=jax.ShapeDtypeStruct((n_pad, D), f32),
        compiler_params=_params("arbitrary", "arbitrary"),
        name="moe_down",
    )(blk_e, n_used, act, dn_w, dn_b.reshape(L, E, 1, D))


def _combine_kernel(dcur_ref, dnxt_ref, w_ref, x_ref, g_ref, ys_ref, o_ref, buf, sem, *, rg, n_tiles):
    tt, d = x_ref.shape
    top_k = w_ref.shape[1]
    i = pl.program_id(0)

    def row_copy(src_row, slot, k, r):
        return pltpu.make_async_copy(ys_ref.at[pl.ds(src_row, 1)], buf.at[slot, k, pl.ds(r, 1)], sem.at[slot])

    def issue(d_ref, slot):
        def body(r, c):
            for k in range(top_k):
                row_copy(d_ref[r * top_k + k], slot, k, r).start()
            return c
        lax.fori_loop(0, tt, body, 0)

    @pl.when(i == 0)
    def _():
        issue(dcur_ref, 0)

    @pl.when(i + 1 < n_tiles)
    def _():
        issue(dnxt_ref, (i + 1) % 2)

    slot = i % 2

    def drain(r, c):
        for k in range(top_k):
            row_copy(0, slot, k, r).wait()
        return c

    lax.fori_loop(0, tt, drain, 0)
    w = w_ref[...]
    f = w[:, 0:1] * buf[slot, 0]
    for k in range(1, top_k):
        f = f + w[:, k:k + 1] * buf[slot, k]
    upd = (f.reshape(tt // rg, rg, d) * g_ref[...]).reshape(tt, d)
    o_ref[...] = x_ref[...] + upd


def _combine(ys, dest_flat, topw, x, mod_g, gate_col, rg):
    T, D = x.shape
    K = topw.shape[1]
    tt = _pick(T, COMBINE_ROWS, rg)
    n_tiles = T // tt
    return pl.pallas_call(
        functools.partial(_combine_kernel, rg=rg, n_tiles=n_tiles),
        grid=(n_tiles,),
        in_specs=[
            pl.BlockSpec((tt * K,), lambda i: (i,), memory_space=pltpu.SMEM),
            pl.BlockSpec((tt * K,), lambda i: (jnp.minimum(i + 1, n_tiles - 1),), memory_space=pltpu.SMEM),
            pl.BlockSpec((tt, K), lambda i: (i, 0)),
            pl.BlockSpec((tt, D), lambda i: (i, 0)),
            pl.BlockSpec((tt // rg, 1, D), lambda i: (i, 0, gate_col)),
            pl.BlockSpec(memory_space=pl.ANY),
        ],
        out_specs=pl.BlockSpec((tt, D), lambda i: (i, 0)),
        out_shape=jax.ShapeDtypeStruct((T, D), f32),
        scratch_shapes=[pltpu.VMEM((2, K, tt, D), f32), pltpu.SemaphoreType.DMA((2,))],
        compiler_params=_params("arbitrary"),
        name="moe_combine",
    )(dest_flat, dest_flat, topw, x, mod_g, ys)


def _moe_layer(x, nw, mod_g, rw, rb, gu_w, gu_b, dn_w, dn_b, layer, rg, tile):
    T, D = x.shape
    E = rw.shape[1]
    bm = MOE_BLOCK
    hp, topi, topw, sel, cnt = _norm_route(x, nw, mod_g, 3, 4, rw, rb, rg, tile)
    cnt_i = cnt[:, 0, :].astype(jnp.int32)
    counts = jnp.sum(cnt_i, axis=0)
    padded = (counts + bm - 1) // bm * bm
    pad_end = jnp.cumsum(padded)
    pad_start = pad_end - padded
    base = (pad_start[None, :] + jnp.cumsum(cnt_i, axis=0) - cnt_i).astype(f32)[:, None, :]
    n_blocks = -(-(T * TOP_K + E * (bm - 1)) // bm)
    blk_start = jnp.arange(n_blocks, dtype=jnp.int32) * bm
    blk_e = jnp.sum((pad_end[None, :] <= blk_start[:, None]).astype(jnp.int32), axis=1)
    blk_e = jnp.minimum(blk_e, E - 1)
    n_used = (pad_end[-1:] // bm).astype(jnp.int32)
    dest = _positions(sel, topi, base, tile).reshape(T * TOP_K)
    xs = _scatter_rows(hp, dest, pad_end.astype(jnp.int32), counts, n_blocks * bm, tile, bm)
    act = _moe_gate_up(xs, blk_e, n_used, gu_w, gu_b, layer, bm)
    ys = _moe_down(act, blk_e, n_used, dn_w, dn_b, layer, bm)
    return _combine(ys, dest, topw, x, mod_g, 5, rg)


def kernel(x_prompt, x_sample, c_prompt, c_sample, state_conv, cache_k, cache_v, ada_w, ada_b, norm_mix, norm_ffn, norm_final, conv_pw1_w, conv_pw1_b, conv_dw_w, conv_dw_b, conv_ln_g, conv_ln_b, conv_pw2_w, conv_pw2_b, attn_qkv_w, attn_qkv_b, attn_o_w, attn_o_b, attn_sinks, router_w, router_b, moe_gu_w, moe_gu_b, moe_dn_w, moe_dn_b):
    B, S, D = x_prompt.shape
    Bs, Ts, _ = x_sample.shape
    depth = ada_w.shape[0]
    n_kv, hd = cache_k.shape[3], cache_k.shape[4]
    n_heads = attn_sinks.shape[1]
    group = n_heads // n_kv
    w_cache = cache_k.shape[2]
    Tp, Tsm = B * S, Bs * Ts
    T = Tp + Tsm
    rg = Ts
    assert S % rg == 0 and rg % 8 == 0 and rg == CONV_ROWS and n_kv % 2 == 0 and 2 * hd == LANES
    tile = _pick(Tsm, ROW_TILE, rg)
    assert Tp % tile == 0

    x = jnp.concatenate([x_prompt.reshape(Tp, D), x_sample.reshape(Tsm, D)], axis=0)
    n_streams = B + Bs
    c_all = jnp.concatenate([c_prompt, c_sample], axis=0)
    c_pad = jnp.pad(c_all, ((0, -n_streams % STREAM_ROWS_PAD), (0, 0)))
    mod = _ada(c_pad, ada_w, ada_b)
    stream_of_group = jnp.concatenate([
        jnp.repeat(jnp.arange(B, dtype=jnp.int32), S // rg),
        B + jnp.repeat(jnp.arange(Bs, dtype=jnp.int32), Ts // rg)])
    conv_states, k_states, v_states = [], [], []
    for i in range(depth):
        mod_g = mod[i][stream_of_group][:, None, :]
        j = i // 2
        h = _norm_mod(x, norm_mix[i], mod_g, 0, 1, rg, tile)
        if i % 2 == 0:
            g = _glu_matmul(h, conv_pw1_w[j].astype(bf16), conv_pw1_b[j])
            hist = jnp.pad(state_conv[j], ((0, 0), (rg - (CONV_WIDTH - 1), 0), (0, 0))).reshape(Bs * rg, D)
            a = _conv_module(g, hist, conv_dw_w[j], conv_dw_b[j], conv_ln_g[j], conv_ln_b[j], Tp, S)
            x = _residual_matmul(a[None], conv_pw2_w[j].astype(bf16)[None], conv_pw2_b[j], x, mod_g, 2, rg,
                                 "pw2_residual")
            keep = CONV_WIDTH - 1
            conv_states.append((g[:Tp].reshape(B, S, D)[:, S - keep:], g[Tp:].reshape(Bs, Ts, D)[:, Ts - keep:]))
        else:
            qd = n_heads * hd
            kd = n_kv * hd
            wq = attn_qkv_w[j][:, :qd].reshape(D, n_kv, group, hd).transpose(0, 2, 1, 3).reshape(D, qd)
            bq = attn_qkv_b[j][:qd].reshape(n_kv, group, hd).transpose(1, 0, 2).reshape(qd)
            q_r = _linear_grouped_out(h, wq.astype(bf16), bq, group, bf16, "q_proj")
            kv = _linear(h, attn_qkv_w[j][:, qd:].astype(bf16), attn_qkv_b[j][qd:], f32, "kv_proj")
            o_r = _attn_prompt(q_r, kv, attn_sinks[j], n_kv, hd, S, Tp, WINDOW, CHUNK)
            kv_new = kv[Tp:].reshape(Bs, Ts, 2 * kd)
            cache = jnp.concatenate([cache_k[j].reshape(Bs, w_cache, kd), cache_v[j].reshape(Bs, w_cache, kd)], axis=-1)
            kv_all = jnp.concatenate([cache, kv_new], axis=1)
            o_r = _attn_sample(q_r, kv_all.reshape(Bs * (w_cache + Ts), 2 * kd), attn_sinks[j], o_r,
                               n_kv, hd, Ts, Tp, Bs)
            wo = attn_o_w[j].reshape(n_kv, group, hd, D).transpose(1, 0, 2, 3).reshape(group, kd, D)
            x = _residual_matmul(o_r, wo.astype(bf16), attn_o_b[j], x, mod_g, 2, rg, "o_residual")
            kvp = kv[:Tp].reshape(B, S, 2, n_kv, hd)[:, S - WINDOW:]
            kvs = kv_all[:, Ts:].reshape(Bs, w_cache, 2, n_kv, hd)
            k_states.append((kvp[:, :, 0], kvs[:, :, 0]))
            v_states.append((kvp[:, :, 1], kvs[:, :, 1]))
        x = _moe_layer(x, norm_ffn[i], mod_g, router_w[i], router_b[i], moe_gu_w, moe_gu_b,
                       moe_dn_w, moe_dn_b, i, rg, tile)
    y_prompt = _final_norm(x, norm_final, tile, 0, Tp)
    y_sample = _final_norm(x, norm_final, tile, Tp, Tsm)
    stack = lambda states, idx: jnp.stack([s[idx] for s in states])
    return (y_prompt.reshape(B, S, D), y_sample.reshape(Bs, Ts, D),
            stack(conv_states, 0), stack(conv_states, 1),
            stack(k_states, 0), stack(v_states, 0), stack(k_states, 1), stack(v_states, 1))
```

```python
import functools

import jax
import jax.numpy as jnp
from jax import lax
from jax.experimental import pallas as pl
from jax.experimental.pallas import tpu as pltpu

CHUNK = 64
WINDOW = 128
CONV_WIDTH = 31
TOP_K = 4
SWIGLU_LIMIT = 7.0
SWIGLU_ALPHA = 1.702
EPS = 1e-5
NEG_INF = -1e30

LANES = 128
SUBLANES = 8
STREAM_ROWS_PAD = 16
VMEM_LIMIT_BYTES = 56 * 1024 * 1024
ROW_TILE = 256
MM_ROW_CAP = 1536
MM_COL_TILE = 512
MOE_BLOCK = 512
MOE_GU_COL_TILE = 512
MOE_DN_COL_TILE = 1024
CONV_ROWS = 32
CONV_COL_CHUNK = 512
ATTN_ROWS = 256
ATTN_INTERLEAVE = 4
COMBINE_ROWS = 64
COMBINE_COL_CHUNK = 512
ADA_COL_TILE = 768

f32 = jnp.float32
bf16 = jnp.bfloat16


def _pick(total, cap, mult):
    best = None
    for t in range(mult, min(total, cap) + 1, mult):
        if total % t == 0:
            best = t
    assert best is not None, (total, cap, mult)
    return best


def _params(*sem):
    return pltpu.CompilerParams(dimension_semantics=sem, vmem_limit_bytes=VMEM_LIMIT_BYTES)


def _dot(a, b):
    return jnp.dot(a, b, preferred_element_type=f32)


def _ada_kernel(c_ref, w_ref, b_ref, o_ref):
    c = c_ref[...]
    s = (c * jax.nn.sigmoid(c)).astype(bf16)
    o_ref[...] = _dot(s, w_ref[...].astype(bf16)) + b_ref[...]


def _ada(c_pad, ada_w, ada_b):
    L, D, N = ada_w.shape
    R = c_pad.shape[0]
    tn = _pick(N, ADA_COL_TILE, LANES)
    return pl.pallas_call(
        _ada_kernel,
        grid=(L, N // tn),
        in_specs=[
            pl.BlockSpec((R, D), lambda l, n: (0, 0)),
            pl.BlockSpec((None, D, tn), lambda l, n: (l, 0, n)),
            pl.BlockSpec((None, 1, tn), lambda l, n: (l, 0, n)),
        ],
        out_specs=pl.BlockSpec((None, R, tn), lambda l, n: (l, 0, n)),
        out_shape=jax.ShapeDtypeStruct((L, R, N), f32),
        compiler_params=_params("arbitrary", "arbitrary"),
        name="ada",
    )(c_pad, ada_w, ada_b.reshape(L, 1, N))


def _modulated_norm(x, nw, scale, shift, rg):
    t, d = x.shape
    y = x * lax.rsqrt(jnp.mean(x * x, axis=-1, keepdims=True) + EPS) * nw
    h = y.reshape(t // rg, rg, d) * (1.0 + scale) + shift
    return h.reshape(t, d)


def _norm_kernel(x_ref, nw_ref, sc_ref, sh_ref, h_ref, *, rg):
    h = _modulated_norm(x_ref[...], nw_ref[...], sc_ref[...], sh_ref[...], rg)
    h_ref[...] = h.astype(h_ref.dtype)


def _norm_mod(x, nw, mod_g, shift_col, scale_col, rg, tile):
    T, D = x.shape
    gt = tile // rg
    return pl.pallas_call(
        functools.partial(_norm_kernel, rg=rg),
        grid=(T // tile,),
        in_specs=[
            pl.BlockSpec((tile, D), lambda i: (i, 0)),
            pl.BlockSpec((1, D), lambda i: (0, 0)),
            pl.BlockSpec((gt, 1, D), lambda i: (i, 0, scale_col)),
            pl.BlockSpec((gt, 1, D), lambda i: (i, 0, shift_col)),
        ],
        out_specs=pl.BlockSpec((tile, D), lambda i: (i, 0)),
        out_shape=jax.ShapeDtypeStruct((T, D), bf16),
        compiler_params=_params("arbitrary"),
        name="norm_mod",
    )(x, nw.reshape(1, D), mod_g, mod_g)


def _pack_bf16_pairs(h):
    d = h.shape[1]
    bits = pltpu.bitcast(h.astype(bf16).astype(f32), jnp.uint32)
    return (bits[:, : d // 2] >> 16) | (bits[:, d // 2:] & jnp.uint32(0xFFFF0000))


def _unpack_bf16_pairs(w):
    lo = pltpu.bitcast(w << 16, f32).astype(bf16)
    hi = pltpu.bitcast(w & jnp.uint32(0xFFFF0000), f32).astype(bf16)
    return lo, hi


def _norm_route_kernel(x_ref, nw_ref, sc_ref, sh_ref, rw_ref, rb_ref,
                       hp_ref, topi_ref, topw_ref, sel_ref, cnt_ref, *, rg, top_k):
    h = _modulated_norm(x_ref[...], nw_ref[...], sc_ref[...], sh_ref[...], rg)
    hp_ref[...] = _pack_bf16_pairs(h)
    logits = _dot(h.astype(bf16), rw_ref[...]) + rb_ref[...]
    t, e = logits.shape
    lane = lax.broadcasted_iota(jnp.int32, (t, e), 1).astype(f32)
    kcol = lax.broadcasted_iota(jnp.int32, (t, top_k), 1)
    sel = jnp.zeros((t, e), f32)
    topi = jnp.zeros((t, top_k), f32)
    topv = jnp.zeros((t, top_k), f32)
    work = logits
    for k in range(top_k):
        m = jnp.max(work, axis=-1, keepdims=True)
        idx = jnp.min(jnp.where(work == m, lane, float(e)), axis=-1, keepdims=True)
        hit = lane == idx
        sel = sel + hit.astype(f32)
        topi = jnp.where(kcol == k, idx, topi)
        topv = jnp.where(kcol == k, m, topv)
        work = jnp.where(hit, -jnp.inf, work)
    p = jnp.exp(topv - topv[:, 0:1])
    topw_ref[...] = p / jnp.sum(p, axis=-1, keepdims=True)
    topi_ref[...] = topi.astype(jnp.int32)
    sel_ref[...] = sel.astype(sel_ref.dtype)
    cnt_ref[...] = jnp.sum(sel, axis=0, keepdims=True)


def _norm_route(x, nw, mod_g, shift_col, scale_col, rw, rb, rg, tile):
    T, D = x.shape
    E = rw.shape[1]
    gt = tile // rg
    nt = T // tile
    return pl.pallas_call(
        functools.partial(_norm_route_kernel, rg=rg, top_k=TOP_K),
        grid=(nt,),
        in_specs=[
            pl.BlockSpec((tile, D), lambda i: (i, 0)),
            pl.BlockSpec((1, D), lambda i: (0, 0)),
            pl.BlockSpec((gt, 1, D), lambda i: (i, 0, scale_col)),
            pl.BlockSpec((gt, 1, D), lambda i: (i, 0, shift_col)),
            pl.BlockSpec((D, E), lambda i: (0, 0)),
            pl.BlockSpec((1, E), lambda i: (0, 0)),
        ],
        out_specs=[
            pl.BlockSpec((tile, D // 2), lambda i: (i, 0)),
            pl.BlockSpec((tile, TOP_K), lambda i: (i, 0)),
            pl.BlockSpec((tile, TOP_K), lambda i: (i, 0)),
            pl.BlockSpec((tile, E), lambda i: (i, 0)),
            pl.BlockSpec((None, 1, E), lambda i: (i, 0, 0)),
        ],
        out_shape=[
            jax.ShapeDtypeStruct((T, D // 2), jnp.uint32),
            jax.ShapeDtypeStruct((T, TOP_K), jnp.int32),
            jax.ShapeDtypeStruct((T, TOP_K), f32),
            jax.ShapeDtypeStruct((T, E), bf16),
            jax.ShapeDtypeStruct((nt, 1, E), f32),
        ],
        compiler_params=_params("arbitrary"),
        name="norm_route",
    )(x, nw.reshape(1, D), mod_g, mod_g, rw.astype(bf16), rb.reshape(1, E))


def _final_norm_kernel(x_ref, nw_ref, o_ref):
    x = x_ref[...]
    o_ref[...] = x * lax.rsqrt(jnp.mean(x * x, axis=-1, keepdims=True) + EPS) * nw_ref[...]


def _final_norm(x, nw, tile, row0, n_rows):
    D = x.shape[1]
    blk0 = row0 // tile
    return pl.pallas_call(
        _final_norm_kernel,
        grid=(n_rows // tile,),
        in_specs=[pl.BlockSpec((tile, D), lambda i: (blk0 + i, 0)), pl.BlockSpec((1, D), lambda i: (0, 0))],
        out_specs=pl.BlockSpec((tile, D), lambda i: (i, 0)),
        out_shape=jax.ShapeDtypeStruct((n_rows, D), f32),
        compiler_params=_params("arbitrary"),
        name="final_norm",
    )(x, nw.reshape(1, D))


def _glu_kernel(x_ref, wa_ref, wb_ref, ba_ref, bb_ref, o_ref):
    x = x_ref[...]
    a = _dot(x, wa_ref[...]) + ba_ref[...]
    b = _dot(x, wb_ref[...]) + bb_ref[...]
    o_ref[...] = a * jax.nn.sigmoid(b)


def _glu_matmul(x, w, b):
    T, K = x.shape
    N = w.shape[1] // 2
    tm = _pick(T, MM_ROW_CAP, 32)
    tn = _pick(N, MM_COL_TILE, LANES)
    nb = N // tn
    b2 = b.reshape(1, 2 * N)
    return pl.pallas_call(
        _glu_kernel,
        grid=(T // tm, nb),
        in_specs=[
            pl.BlockSpec((tm, K), lambda m, n: (m, 0)),
            pl.BlockSpec((K, tn), lambda m, n: (0, n)),
            pl.BlockSpec((K, tn), lambda m, n: (0, n + nb)),
            pl.BlockSpec((1, tn), lambda m, n: (0, n)),
            pl.BlockSpec((1, tn), lambda m, n: (0, n + nb)),
        ],
        out_specs=pl.BlockSpec((tm, tn), lambda m, n: (m, n)),
        out_shape=jax.ShapeDtypeStruct((T, N), f32),
        compiler_params=_params("arbitrary", "arbitrary"),
        name="pw1_glu",
    )(x, w, w, b2, b2)


def _linear_kernel(x_ref, w_ref, b_ref, o_ref):
    o_ref[...] = (_dot(x_ref[...], w_ref[...]) + b_ref[...]).astype(o_ref.dtype)


def _linear_grouped_out(x, w, b, groups, out_dtype, name):
    T, K = x.shape
    ng = w.shape[1] // groups
    tm = _pick(T, MM_ROW_CAP, 32)
    return pl.pallas_call(
        _linear_kernel,
        grid=(T // tm, groups),
        in_specs=[
            pl.BlockSpec((tm, K), lambda m, n: (m, 0)),
            pl.BlockSpec((K, ng), lambda m, n: (0, n)),
            pl.BlockSpec((1, ng), lambda m, n: (0, n)),
        ],
        out_specs=pl.BlockSpec((None, tm, ng), lambda m, n: (n, m, 0)),
        out_shape=jax.ShapeDtypeStruct((groups, T, ng), out_dtype),
        compiler_params=_params("arbitrary", "arbitrary"),
        name=name,
    )(x, w, b.reshape(1, -1))


def _linear(x, w, b, out_dtype, name):
    T, K = x.shape
    N = w.shape[1]
    tm = _pick(T, MM_ROW_CAP, 32)
    tn = _pick(N, MM_COL_TILE, LANES)
    return pl.pallas_call(
        _linear_kernel,
        grid=(T // tm, N // tn),
        in_specs=[
            pl.BlockSpec((tm, K), lambda m, n: (m, 0)),
            pl.BlockSpec((K, tn), lambda m, n: (0, n)),
            pl.BlockSpec((1, tn), lambda m, n: (0, n)),
        ],
        out_specs=pl.BlockSpec((tm, tn), lambda m, n: (m, n)),
        out_shape=jax.ShapeDtypeStruct((T, N), out_dtype),
        compiler_params=_params("arbitrary", "arbitrary"),
        name=name,
    )(x, w, b.reshape(1, -1))


def _residual_kernel(l_ref, w_ref, b_ref, xr_ref, g_ref, o_ref, *, rg):
    groups = l_ref.shape[0]
    acc = _dot(l_ref[0], w_ref[0])
    for g in range(1, groups):
        acc = acc + _dot(l_ref[g], w_ref[g])
    y = acc + b_ref[...]
    tm, tn = y.shape
    upd = (y.reshape(tm // rg, rg, tn) * g_ref[...]).reshape(tm, tn)
    o_ref[...] = xr_ref[...] + upd


def _residual_matmul(lhs, w, b, x_res, mod_g, gate_col, rg, name):
    G, T, Kg = lhs.shape
    N = w.shape[2]
    tm = _pick(T, MM_ROW_CAP, rg)
    tn = _pick(N, MM_COL_TILE, LANES)
    nb = N // tn
    return pl.pallas_call(
        functools.partial(_residual_kernel, rg=rg),
        grid=(T // tm, nb),
        in_specs=[
            pl.BlockSpec((G, tm, Kg), lambda m, n: (0, m, 0)),
            pl.BlockSpec((G, Kg, tn), lambda m, n: (0, 0, n)),
            pl.BlockSpec((1, tn), lambda m, n: (0, n)),
            pl.BlockSpec((tm, tn), lambda m, n: (m, n)),
            pl.BlockSpec((tm // rg, 1, tn), lambda m, n: (m, 0, gate_col * nb + n)),
        ],
        out_specs=pl.BlockSpec((tm, tn), lambda m, n: (m, n)),
        out_shape=jax.ShapeDtypeStruct((T, N), f32),
        compiler_params=_params("arbitrary", "arbitrary"),
        name=name,
    )(lhs, w, b.reshape(1, N), x_res, mod_g)


def _conv_kernel(g_ref, prev_ref, hist_ref, w_ref, b_ref, lg_ref, lb_ref, o_ref, phase, ybuf, *,
                 prompt_steps, steps_per_seq, col_chunk):
    tt, d = g_ref.shape
    hb = prev_ref.shape[0]
    taps = w_ref.shape[0]
    i = pl.program_id(0)
    prev = jnp.where((i % steps_per_seq) == 0, 0.0, prev_ref[...])
    phase[0, 0:hb, :] = jnp.where(i >= prompt_steps, hist_ref[...], prev)
    phase[0, hb:hb + tt, :] = g_ref[...]
    lo, hi = hb - (taps - 1), hb
    steps = {b: [a for a in range(hi // SUBLANES + 1) if lo <= SUBLANES * a + b <= hi] for b in range(SUBLANES)}
    for b in range(1, SUBLANES):
        if steps[b]:
            rows = SUBLANES * steps[b][-1] + tt
            phase[b, 0:rows, :] = phase[0, b:b + rows, :]
    for c in range(d // col_chunk):
        cs = slice(c * col_chunk, (c + 1) * col_chunk)
        acc = jnp.broadcast_to(b_ref[:, cs], (tt, col_chunk)).reshape(tt // SUBLANES, SUBLANES, col_chunk)
        for b in range(SUBLANES):
            if not steps[b]:
                continue
            a0 = steps[b][0]
            span = phase[b, SUBLANES * a0:SUBLANES * steps[b][-1] + tt, cs]
            for a in steps[b]:
                k = SUBLANES * a + b - lo
                rows = span[SUBLANES * (a - a0):SUBLANES * (a - a0) + tt, :]
                acc = acc + rows.reshape(tt // SUBLANES, SUBLANES, col_chunk) * w_ref[k, :, cs]
        ybuf[:, cs] = acc.reshape(tt, col_chunk)
    y = ybuf[...]
    mu = jnp.mean(y, axis=-1, keepdims=True)
    yc = y - mu
    var = jnp.mean(yc * yc, axis=-1, keepdims=True)
    yn = yc * lax.rsqrt(var + EPS) * lg_ref[...] + lb_ref[...]
    o_ref[...] = (yn * jax.nn.sigmoid(yn)).astype(o_ref.dtype)


def _conv_module(g, hist, dw_w, dw_b, ln_g, ln_b, n_prompt_rows, seq):
    T, D = g.shape
    tt = CONV_ROWS
    taps = dw_w.shape[0]
    prompt_steps = n_prompt_rows // tt
    last_hist = hist.shape[0] // tt - 1
    kern = functools.partial(_conv_kernel, prompt_steps=prompt_steps, steps_per_seq=seq // tt,
                             col_chunk=_pick(D, CONV_COL_CHUNK, LANES))
    vec = pl.BlockSpec((1, D), lambda i: (0, 0))
    return pl.pallas_call(
        kern,
        grid=(T // tt,),
        in_specs=[
            pl.BlockSpec((tt, D), lambda i: (i, 0)),
            pl.BlockSpec((tt, D), lambda i: (jnp.maximum(i - 1, 0), 0)),
            pl.BlockSpec((tt, D), lambda i: (jnp.clip(i - prompt_steps, 0, last_hist), 0)),
            pl.BlockSpec((taps, SUBLANES, D), lambda i: (0, 0, 0)),
            vec, vec, vec,
        ],
        out_specs=pl.BlockSpec((tt, D), lambda i: (i, 0)),
        out_shape=jax.ShapeDtypeStruct((T, D), bf16),
        scratch_shapes=[pltpu.VMEM((SUBLANES, 2 * tt, D), f32), pltpu.VMEM((tt, D), f32)],
        compiler_params=_params("arbitrary"),
        name="conv_ln_swish",
    )(g, g, hist, jnp.broadcast_to(dw_w[:, None, :], (taps, SUBLANES, D)),
      dw_b.reshape(1, D), ln_g.reshape(1, D), ln_b.reshape(1, D))


def _attn_prompt_kernel(*refs, live_tiles, **kw):
    o_ref = refs[7]
    live = pl.program_id(1) < live_tiles

    @pl.when(live)
    def _():
        _attn_kernel(*refs, **kw)

    @pl.when(jnp.logical_not(live))
    def _():
        o_ref[...] = jnp.zeros_like(o_ref)


def _attn_kernel(*refs, n_chunks, cq, has_halo, scale, chunks_per_seq, width=ATTN_INTERLEAVE):
    if has_halo:
        q_ref, km_ref, kh_ref, vm_ref, vh_ref, bias_ref, sink_ref, o_ref, kbuf, vbuf = refs
        hr = kh_ref.shape[0]
        kbuf[0:hr, :] = kh_ref[...]
        vbuf[0:hr, :] = vh_ref[...]
    else:
        q_ref, km_ref, vm_ref, bias_ref, sink_ref, o_ref, kbuf, vbuf = refs
        hr = 0
    mr = km_ref.shape[0]
    kbuf[hr:hr + mr, :] = km_ref[...]
    vbuf[hr:hr + mr, :] = vm_ref[...]
    G = q_ref.shape[0]
    pw = q_ref.shape[2]
    hd = pw // 2
    nb = bias_ref.shape[2]
    rows = G * cq
    lane_k = lax.broadcasted_iota(jnp.int32, (nb, pw), 1)
    lane_o = lax.broadcasted_iota(jnp.int32, (rows, pw), 1)
    col = lax.broadcasted_iota(jnp.int32, (rows, nb), 1)
    first_chunk = (pl.program_id(1) % (chunks_per_seq // n_chunks)) * n_chunks if has_halo else None
    heads = range(2)
    sinks = [sink_ref[hh] for hh in heads]
    for j0 in range(0, n_chunks, width):
        its = [(j, hh) for j in range(j0, min(j0 + width, n_chunks)) for hh in heads]
        vps = {j: vbuf[j * cq:j * cq + nb, :].astype(bf16) for j, _ in its}
        ss = []
        for j, hh in its:
            q = q_ref[:, j * cq:(j + 1) * cq, :].reshape(rows, pw)
            kp = kbuf[j * cq:j * cq + nb, :]
            in_head = (lane_k < hd) if hh == 0 else (lane_k >= hd)
            kh = jnp.where(in_head, kp, 0.0).astype(bf16)
            s = lax.dot_general(q, kh, (((1,), (1,)), ((), ())), preferred_element_type=f32)
            s = s * scale + bias_ref[hh]
            if has_halo:
                kpos0 = (first_chunk + j) * cq - (nb - cq)
                s = jnp.where(col + kpos0 >= 0, s, NEG_INF)
            ss.append(s)
        ms = [jnp.maximum(jnp.max(s, axis=-1, keepdims=True), sinks[hh]) for s, (j, hh) in zip(ss, its)]
        ps = [jnp.exp(s - m) for s, m in zip(ss, ms)]
        dens = [jnp.sum(p, axis=-1, keepdims=True) + jnp.exp(sinks[hh] - m) for p, m, (j, hh) in zip(ps, ms, its)]
        outs = [_dot(p.astype(bf16), vps[j]) / den for p, den, (j, hh) in zip(ps, dens, its)]
        for idx in range(0, len(its), 2):
            j = its[idx][0]
            o_pair = jnp.where(lane_o < hd, outs[idx], outs[idx + 1])
            o_ref[:, j * cq:(j + 1) * cq, :] = o_pair.reshape(G, cq, pw).astype(o_ref.dtype)


def _attn_tables(sinks, n_kv, group, cq, nb, window):
    n_heads = n_kv * group
    slopes = jnp.exp2(-8.0 * jnp.arange(1, n_heads + 1, dtype=f32) / n_heads).reshape(n_kv, group)
    qi = jnp.arange(cq, dtype=jnp.int32)[:, None]
    si = jnp.arange(nb, dtype=jnp.int32)[None, :]
    dist = jnp.abs(window + qi - si).astype(f32)
    bias = -slopes[:, :, None, None] * dist[None, None]
    bias = bias.reshape(n_kv, group * cq, nb)
    sink = jnp.broadcast_to(sinks.astype(f32).reshape(n_kv, group, 1, 1), (n_kv, group, cq, 1))
    return bias, sink.reshape(n_kv, group * cq, 1)


def _attn_prompt(q_r, kv, sinks, n_kv, hd, seq, n_rows, window, chunk):
    G, T, Q = q_r.shape
    pw = 2 * hd
    pairs = n_kv // 2
    tr = _pick(seq, ATTN_ROWS, max(chunk, window))
    n_chunks = tr // chunk
    nb = window + chunk
    bias, sink = _attn_tables(sinks, n_kv, G, chunk, nb, window)
    hpb = tr // window
    assert T % tr == 0 and n_rows % tr == 0
    kern = functools.partial(_attn_prompt_kernel, live_tiles=n_rows // tr, n_chunks=n_chunks, cq=chunk,
                             has_halo=True, scale=hd ** -0.5, chunks_per_seq=seq // chunk)
    halo = lambda off: (lambda p, i: (jnp.maximum(i * hpb - 1, 0), off + p))
    return pl.pallas_call(
        kern,
        grid=(pairs, T // tr),
        in_specs=[
            pl.BlockSpec((G, tr, pw), lambda p, i: (0, i, p)),
            pl.BlockSpec((tr, pw), lambda p, i: (i, p)),
            pl.BlockSpec((window, pw), halo(0)),
            pl.BlockSpec((tr, pw), lambda p, i: (i, pairs + p)),
            pl.BlockSpec((window, pw), halo(pairs)),
            pl.BlockSpec((2, G * chunk, nb), lambda p, i: (p, 0, 0)),
            pl.BlockSpec((2, G * chunk, 1), lambda p, i: (p, 0, 0)),
        ],
        out_specs=pl.BlockSpec((G, tr, pw), lambda p, i: (0, i, p)),
        out_shape=jax.ShapeDtypeStruct((G, T, Q), bf16),
        scratch_shapes=[pltpu.VMEM((window + tr, pw), f32), pltpu.VMEM((window + tr, pw), f32)],
        compiler_params=_params("arbitrary", "arbitrary"),
        name="attn_prompt",
    )(q_r, kv, kv, kv, kv, bias, sink)


def _attn_sample(q_r, kv_all, sinks, o_prev, n_kv, hd, t_new, row0, n_streams):
    G, T, Q = q_r.shape
    pw = 2 * hd
    pairs = n_kv // 2
    nb = kv_all.shape[0] // n_streams
    bias, sink = _attn_tables(sinks, n_kv, G, t_new, nb, nb - t_new)
    kern = functools.partial(_attn_kernel, n_chunks=1, cq=t_new, has_halo=False,
                             scale=hd ** -0.5, chunks_per_seq=1)
    blk0 = row0 // t_new
    return pl.pallas_call(
        lambda *refs: kern(*refs[:5], *refs[6:]),
        grid=(pairs, n_streams),
        in_specs=[
            pl.BlockSpec((G, t_new, pw), lambda p, i: (0, blk0 + i, p)),
            pl.BlockSpec((nb, pw), lambda p, i: (i, p)),
            pl.BlockSpec((nb, pw), lambda p, i: (i, pairs + p)),
            pl.BlockSpec((2, G * t_new, nb), lambda p, i: (p, 0, 0)),
            pl.BlockSpec((2, G * t_new, 1), lambda p, i: (p, 0, 0)),
            pl.BlockSpec(memory_space=pl.ANY),
        ],
        out_specs=pl.BlockSpec((G, t_new, pw), lambda p, i: (0, blk0 + i, p)),
        out_shape=jax.ShapeDtypeStruct((G, T, Q), bf16),
        scratch_shapes=[pltpu.VMEM((nb, pw), f32), pltpu.VMEM((nb, pw), f32)],
        input_output_aliases={5: 0},
        compiler_params=_params("arbitrary", "arbitrary"),
        name="attn_sample",
    )(q_r, kv_all, kv_all, bias, sink, o_prev)


def _pos_kernel(sel_ref, topi_ref, base_ref, dest_ref):
    t, e = sel_ref.shape
    top_k = topi_ref.shape[1]
    r = lax.broadcasted_iota(jnp.int32, (t, t), 0)
    c = lax.broadcasted_iota(jnp.int32, (t, t), 1)
    earlier = (r > c).astype(bf16)
    rank = _dot(earlier, sel_ref[...]) + base_ref[...]
    lane = lax.broadcasted_iota(jnp.int32, (t, e), 1)
    kcol = lax.broadcasted_iota(jnp.int32, (t, top_k), 1)
    topi = topi_ref[...]
    dest = jnp.zeros((t, top_k), f32)
    for k in range(top_k):
        dk = jnp.sum(jnp.where(lane == topi[:, k:k + 1], rank, 0.0), axis=-1, keepdims=True)
        dest = jnp.where(kcol == k, dk, dest)
    dest_ref[...] = dest.astype(jnp.int32)


def _positions(sel, topi, base, tile):
    T, E = sel.shape
    K = topi.shape[1]
    return pl.pallas_call(
        _pos_kernel,
        grid=(T // tile,),
        in_specs=[
            pl.BlockSpec((tile, E), lambda i: (i, 0)),
            pl.BlockSpec((tile, K), lambda i: (i, 0)),
            pl.BlockSpec((None, 1, E), lambda i: (i, 0, 0)),
        ],
        out_specs=pl.BlockSpec((tile, K), lambda i: (i, 0)),
        out_shape=jax.ShapeDtypeStruct((T, K), jnp.int32),
        compiler_params=_params("arbitrary"),
        name="moe_positions",
    )(sel, topi, base)


def _scatter_kernel(pe_ref, cn_ref, dest_ref, hp_ref, xs_ref, zbuf, sem, zsem, *, top_k, bm):
    tile = hp_ref.shape[0]
    n_exp = pe_ref.shape[0]

    n_blocks = xs_ref.shape[0] // bm
    n_used = pe_ref[n_exp - 1] // bm

    def zero_block(m):
        return pltpu.make_async_copy(zbuf, xs_ref.at[pl.ds(pl.multiple_of(m * bm, bm), bm)], zsem)

    def zero_tail(e):
        return zero_block(pe_ref[e] // bm - 1)

    @pl.when(pl.program_id(0) == 0)
    def _():
        zbuf[...] = jnp.zeros_like(zbuf)

        def start(e, c):
            @pl.when(cn_ref[e] > 0)
            def _():
                zero_tail(e).start()
            return c

        def wait(e, c):
            @pl.when(cn_ref[e] > 0)
            def _():
                zero_tail(e).wait()
            return c

        def start_unused(m, c):
            zero_block(m).start()
            return c

        def wait_unused(m, c):
            zero_block(m).wait()
            return c

        lax.fori_loop(0, n_exp, start, 0)
        lax.fori_loop(n_used, n_blocks, start_unused, 0)
        lax.fori_loop(0, n_exp, wait, 0)
        lax.fori_loop(n_used, n_blocks, wait_unused, 0)

    def row_copy(r, d):
        return pltpu.make_async_copy(hp_ref.at[pl.ds(r, 1)], xs_ref.at[pl.ds(d, 1)], sem)

    def issue(r, c):
        for k in range(top_k):
            row_copy(r, dest_ref[r * top_k + k]).start()
        return c

    def drain(r, c):
        for k in range(top_k):
            row_copy(r, dest_ref[r * top_k + k]).wait()
        return c

    lax.fori_loop(0, tile, issue, 0)
    lax.fori_loop(0, tile, drain, 0)


def _scatter_rows(hp, dest_flat, pad_end, counts, n_pad, tile, bm):
    T, W = hp.shape
    K = dest_flat.shape[0] // T
    grid_spec = pltpu.PrefetchScalarGridSpec(
        num_scalar_prefetch=2,
        grid=(T // tile,),
        in_specs=[
            pl.BlockSpec((tile * K,), lambda i, pe, cn: (i,), memory_space=pltpu.SMEM),
            pl.BlockSpec((tile, W), lambda i, pe, cn: (i, 0)),
        ],
        out_specs=pl.BlockSpec(memory_space=pl.ANY),
        scratch_shapes=[pltpu.VMEM((bm, W), jnp.uint32), pltpu.SemaphoreType.DMA, pltpu.SemaphoreType.DMA],
    )
    return pl.pallas_call(
        functools.partial(_scatter_kernel, top_k=K, bm=bm),
        grid_spec=grid_spec,
        out_shape=jax.ShapeDtypeStruct((n_pad, W), jnp.uint32),
        compiler_params=_params("arbitrary"),
        name="moe_scatter",
    )(pad_end, counts, dest_flat, hp)


def _expert_step(be_ref, nu_ref):
    m = pl.program_id(1)
    last = nu_ref[0] - 1
    mm = jnp.minimum(m, last)
    prev = jnp.maximum(mm - 1, 0)
    fresh = jnp.logical_or(m == 0, be_ref[mm] != be_ref[prev])
    return m <= last, fresh


def _moe_gu_kernel(be_ref, nu_ref, xs_ref, wg_ref, wu_ref, bg_ref, bu_ref, o_ref, wg_bf, wu_bf):
    live, fresh = _expert_step(be_ref, nu_ref)

    @pl.when(jnp.logical_and(live, fresh))
    def _():
        wg_bf[...] = wg_ref[...].astype(bf16)
        wu_bf[...] = wu_ref[...].astype(bf16)

    @pl.when(live)
    def _():
        lo, hi = _unpack_bf16_pairs(xs_ref[...])
        half = lo.shape[1]
        gate = _dot(lo, wg_bf[0:half, :]) + _dot(hi, wg_bf[half:, :]) + bg_ref[...]
        up = _dot(lo, wu_bf[0:half, :]) + _dot(hi, wu_bf[half:, :]) + bu_ref[...]
        gate = jnp.minimum(gate, SWIGLU_LIMIT)
        up = jnp.clip(up, -SWIGLU_LIMIT, SWIGLU_LIMIT)
        act = (up + 1.0) * (gate * jax.nn.sigmoid(SWIGLU_ALPHA * gate))
        o_ref[...] = act.astype(o_ref.dtype)

    @pl.when(jnp.logical_not(live))
    def _():
        o_ref[...] = jnp.zeros_like(o_ref)


def _moe_gate_up(xs, blk_e, n_used, gu_w, gu_b, layer, bm):
    n_pad, W = xs.shape
    L, E, D, F2 = gu_w.shape
    F = F2 // 2
    tn = _pick(F, MOE_GU_COL_TILE, LANES)
    nb = F // tn
    row = lambda n, m, be, nu: jnp.minimum(m, nu[0] - 1)
    exp = lambda n, m, be, nu: be[jnp.minimum(m, nu[0] - 1)]
    grid_spec = pltpu.PrefetchScalarGridSpec(
        num_scalar_prefetch=2,
        grid=(nb, n_pad // bm),
        in_specs=[
            pl.BlockSpec((bm, W), lambda n, m, be, nu: (row(n, m, be, nu), 0)),
            pl.BlockSpec((None, None, D, tn), lambda n, m, be, nu: (layer, exp(n, m, be, nu), 0, n)),
            pl.BlockSpec((None, None, D, tn), lambda n, m, be, nu: (layer, exp(n, m, be, nu), 0, n + nb)),
            pl.BlockSpec((None, None, 1, tn), lambda n, m, be, nu: (layer, exp(n, m, be, nu), 0, n)),
            pl.BlockSpec((None, None, 1, tn), lambda n, m, be, nu: (layer, exp(n, m, be, nu), 0, n + nb)),
        ],
        out_specs=pl.BlockSpec((bm, tn), lambda n, m, be, nu: (m, n)),
        scratch_shapes=[pltpu.VMEM((D, tn), bf16), pltpu.VMEM((D, tn), bf16)],
    )
    gu_b3 = gu_b.reshape(L, E, 1, F2)
    return pl.pallas_call(
        _moe_gu_kernel,
        grid_spec=grid_spec,
        out_shape=jax.ShapeDtypeStruct((n_pad, F), bf16),
        compiler_params=_params("arbitrary", "arbitrary"),
        name="moe_gate_up",
    )(blk_e, n_used, xs, gu_w, gu_w, gu_b3, gu_b3)


def _moe_dn_kernel(be_ref, nu_ref, a_ref, w_ref, b_ref, o_ref, w_bf):
    live, fresh = _expert_step(be_ref, nu_ref)

    @pl.when(jnp.logical_and(live, fresh))
    def _():
        w_bf[...] = w_ref[...].astype(bf16)

    @pl.when(live)
    def _():
        o_ref[...] = _dot(a_ref[...], w_bf[...]) + b_ref[...]

    @pl.when(jnp.logical_not(live))
    def _():
        o_ref[...] = jnp.zeros_like(o_ref)


def _moe_down(act, blk_e, n_used, dn_w, dn_b, layer, bm):
    n_pad, F = act.shape
    L, E, _, D = dn_w.shape
    tn = _pick(D, MOE_DN_COL_TILE, LANES)
    row = lambda n, m, be, nu: jnp.minimum(m, nu[0] - 1)
    exp = lambda n, m, be, nu: be[jnp.minimum(m, nu[0] - 1)]
    grid_spec = pltpu.PrefetchScalarGridSpec(
        num_scalar_prefetch=2,
        grid=(D // tn, n_pad // bm),
        in_specs=[
            pl.BlockSpec((bm, F), lambda n, m, be, nu: (row(n, m, be, nu), 0)),
            pl.BlockSpec((None, None, F, tn), lambda n, m, be, nu: (layer, exp(n, m, be, nu), 0, n)),
            pl.BlockSpec((None, None, 1, tn), lambda n, m, be, nu: (layer, exp(n, m, be, nu), 0, n)),
        ],
        out_specs=pl.BlockSpec((bm, tn), lambda n, m, be, nu: (m, n)),
        scratch_shapes=[pltpu.VMEM((F, tn), bf16)],
    )
    return pl.pallas_call(
        _moe_dn_kernel,
        grid_spec=grid_spec,
        out_shape=jax.ShapeDtypeStruct((n_pad, D), f32),
        compiler_params=_params("arbitrary", "arbitrary"),
        name="moe_down",
    )(blk_e, n_used, act, dn_w, dn_b.reshape(L, E, 1, D))


def _combine_kernel(dcur_ref, dnxt_ref, w_ref, x_ref, g_ref, ys_ref, o_ref, buf, sem, *, rg, n_tiles):
    tt, d = x_ref.shape
    top_k = w_ref.shape[1]
    i = pl.program_id(0)

    def row_copy(src_row, slot, k, r):
        return pltpu.make_async_copy(ys_ref.at[pl.ds(src_row, 1)], buf.at[slot, k, pl.ds(r, 1)], sem.at[slot])

    def issue(d_ref, slot):
        def body(r, c):
            for k in range(top_k):
                row_copy(d_ref[r * top_k + k], slot, k, r).start()
            return c
        lax.fori_loop(0, tt, body, 0)

    @pl.when(i == 0)
    def _():
        issue(dcur_ref, 0)

    @pl.when(i + 1 < n_tiles)
    def _():
        issue(dnxt_ref, (i + 1) % 2)

    slot = i % 2

    def drain(r, c):
        for k in range(top_k):
            row_copy(0, slot, k, r).wait()
        return c

    lax.fori_loop(0, tt, drain, 0)
    w = w_ref[...]
    cc = _pick(d, COMBINE_COL_CHUNK, LANES)
    for c in range(d // cc):
        cs = slice(c * cc, (c + 1) * cc)
        f = w[:, 0:1] * buf[slot, 0, :, cs]
        for k in range(1, top_k):
            f = f + w[:, k:k + 1] * buf[slot, k, :, cs]
        upd = (f.reshape(tt // rg, rg, cc) * g_ref[:, :, cs]).reshape(tt, cc)
        o_ref[:, cs] = x_ref[:, cs] + upd


def _combine(ys, dest_flat, topw, x, mod_g, gate_col, rg):
    T, D = x.shape
    K = topw.shape[1]
    tt = _pick(T, COMBINE_ROWS, rg)
    n_tiles = T // tt
    return pl.pallas_call(
        functools.partial(_combine_kernel, rg=rg, n_tiles=n_tiles),
        grid=(n_tiles,),
        in_specs=[
            pl.BlockSpec((tt * K,), lambda i: (i,), memory_space=pltpu.SMEM),
            pl.BlockSpec((tt * K,), lambda i: (jnp.minimum(i + 1, n_tiles - 1),), memory_space=pltpu.SMEM),
            pl.BlockSpec((tt, K), lambda i: (i, 0)),
            pl.BlockSpec((tt, D), lambda i: (i, 0)),
            pl.BlockSpec((tt // rg, 1, D), lambda i: (i, 0, gate_col)),
            pl.BlockSpec(memory_space=pl.ANY),
        ],
        out_specs=pl.BlockSpec((tt, D), lambda i: (i, 0)),
        out_shape=jax.ShapeDtypeStruct((T, D), f32),
        scratch_shapes=[pltpu.VMEM((2, K, tt, D), f32), pltpu.SemaphoreType.DMA((2,))],
        compiler_params=_params("arbitrary"),
        name="moe_combine",
    )(dest_flat, dest_flat, topw, x, mod_g, ys)


def _moe_layer(x, nw, mod_g, rw, rb, gu_w, gu_b, dn_w, dn_b, layer, rg, tile):
    T, D = x.shape
    E = rw.shape[1]
    bm = MOE_BLOCK
    hp, topi, topw, sel, cnt = _norm_route(x, nw, mod_g, 3, 4, rw, rb, rg, tile)
    cnt_i = cnt[:, 0, :].astype(jnp.int32)
    counts = jnp.sum(cnt_i, axis=0)
    padded = (counts + bm - 1) // bm * bm
    pad_end = jnp.cumsum(padded)
    pad_start = pad_end - padded
    base = (pad_start[None, :] + jnp.cumsum(cnt_i, axis=0) - cnt_i).astype(f32)[:, None, :]
    n_blocks = -(-(T * TOP_K + E * (bm - 1)) // bm)
    blk_start = jnp.arange(n_blocks, dtype=jnp.int32) * bm
    blk_e = jnp.sum((pad_end[None, :] <= blk_start[:, None]).astype(jnp.int32), axis=1)
    blk_e = jnp.minimum(blk_e, E - 1)
    n_used = (pad_end[-1:] // bm).astype(jnp.int32)
    dest = _positions(sel, topi, base, tile).reshape(T * TOP_K)
    xs = _scatter_rows(hp, dest, pad_end.astype(jnp.int32), counts, n_blocks * bm, tile, bm)
    act = _moe_gate_up(xs, blk_e, n_used, gu_w, gu_b, layer, bm)
    ys = _moe_down(act, blk_e, n_used, dn_w, dn_b, layer, bm)
    return _combine(ys, dest, topw, x, mod_g, 5, rg)


def _tails(a, row0, n_seq, seq_len, keep):
    return jnp.stack([a[row0 + (s + 1) * seq_len - keep:row0 + (s + 1) * seq_len] for s in range(n_seq)])


def kernel(x_prompt, x_sample, c_prompt, c_sample, state_conv, cache_k, cache_v, ada_w, ada_b, norm_mix, norm_ffn, norm_final, conv_pw1_w, conv_pw1_b, conv_dw_w, conv_dw_b, conv_ln_g, conv_ln_b, conv_pw2_w, conv_pw2_b, attn_qkv_w, attn_qkv_b, attn_o_w, attn_o_b, attn_sinks, router_w, router_b, moe_gu_w, moe_gu_b, moe_dn_w, moe_dn_b):
    B, S, D = x_prompt.shape
    Bs, Ts, _ = x_sample.shape
    depth = ada_w.shape[0]
    n_kv, hd = cache_k.shape[3], cache_k.shape[4]
    n_heads = attn_sinks.shape[1]
    group = n_heads // n_kv
    w_cache = cache_k.shape[2]
    Tp, Tsm = B * S, Bs * Ts
    T = Tp + Tsm
    rg = Ts
    assert S % rg == 0 and rg % 8 == 0 and rg == CONV_ROWS and n_kv % 2 == 0 and 2 * hd == LANES
    tile = _pick(Tsm, ROW_TILE, rg)
    assert Tp % tile == 0

    x = jnp.concatenate([x_prompt.reshape(Tp, D), x_sample.reshape(Tsm, D)], axis=0)
    n_streams = B + Bs
    c_all = jnp.concatenate([c_prompt, c_sample], axis=0)
    c_pad = jnp.pad(c_all, ((0, -n_streams % STREAM_ROWS_PAD), (0, 0)))
    mod = _ada(c_pad, ada_w, ada_b)
    stream_of_group = jnp.concatenate([
        jnp.repeat(jnp.arange(B, dtype=jnp.int32), S // rg),
        B + jnp.repeat(jnp.arange(Bs, dtype=jnp.int32), Ts // rg)])
    conv_states, k_states, v_states = [], [], []
    for i in range(depth):
        mod_g = mod[i][stream_of_group][:, None, :]
        j = i // 2
        h = _norm_mod(x, norm_mix[i], mod_g, 0, 1, rg, tile)
        if i % 2 == 0:
            g = _glu_matmul(h, conv_pw1_w[j].astype(bf16), conv_pw1_b[j])
            hist = jnp.pad(state_conv[j], ((0, 0), (rg - (CONV_WIDTH - 1), 0), (0, 0))).reshape(Bs * rg, D)
            a = _conv_module(g, hist, conv_dw_w[j], conv_dw_b[j], conv_ln_g[j], conv_ln_b[j], Tp, S)
            x = _residual_matmul(a[None], conv_pw2_w[j].astype(bf16)[None], conv_pw2_b[j], x, mod_g, 2, rg,
                                 "pw2_residual")
            keep = CONV_WIDTH - 1
            conv_states.append((_tails(g, 0, B, S, keep), _tails(g, Tp, Bs, Ts, keep)))
        else:
            qd = n_heads * hd
            kd = n_kv * hd
            wq = attn_qkv_w[j][:, :qd].reshape(D, n_kv, group, hd).transpose(0, 2, 1, 3).reshape(D, qd)
            bq = attn_qkv_b[j][:qd].reshape(n_kv, group, hd).transpose(1, 0, 2).reshape(qd)
            q_r = _linear_grouped_out(h, wq.astype(bf16), bq, group, bf16, "q_proj")
            kv = _linear(h, attn_qkv_w[j][:, qd:].astype(bf16), attn_qkv_b[j][qd:], f32, "kv_proj")
            o_r = _attn_prompt(q_r, kv, attn_sinks[j], n_kv, hd, S, Tp, WINDOW, CHUNK)
            kv_new = kv[Tp:].reshape(Bs, Ts, 2 * kd)
            cache = jnp.concatenate([cache_k[j].reshape(Bs, w_cache, kd), cache_v[j].reshape(Bs, w_cache, kd)], axis=-1)
            kv_all = jnp.concatenate([cache, kv_new], axis=1)
            o_r = _attn_sample(q_r, kv_all.reshape(Bs * (w_cache + Ts), 2 * kd), attn_sinks[j], o_r,
                               n_kv, hd, Ts, Tp, Bs)
            wo = attn_o_w[j].reshape(n_kv, group, hd, D).transpose(1, 0, 2, 3).reshape(group, kd, D)
            x = _residual_matmul(o_r, wo.astype(bf16), attn_o_b[j], x, mod_g, 2, rg, "o_residual")
            kvp = _tails(kv, 0, B, S, WINDOW).reshape(B, WINDOW, 2, n_kv, hd)
            kvs = kv_all[:, Ts:].reshape(Bs, w_cache, 2, n_kv, hd)
            k_states.append((kvp[:, :, 0], kvs[:, :, 0]))
            v_states.append((kvp[:, :, 1], kvs[:, :, 1]))
        x = _moe_layer(x, norm_ffn[i], mod_g, router_w[i], router_b[i], moe_gu_w, moe_gu_b,
                       moe_dn_w, moe_dn_b, i, rg, tile)
    y_prompt = _final_norm(x, norm_final, tile, 0, Tp)
    y_sample = _final_norm(x, norm_final, tile, Tp, Tsm)
    stack = lambda states, idx: jnp.stack([s[idx] for s in states])
    return (y_prompt.reshape(B, S, D), y_sample.reshape(Bs, Ts, D),
            stack(conv_states, 0), stack(conv_states, 1),
            stack(k_states, 0), stack(v_states, 0), stack(k_states, 1), stack(v_states, 1))
```

```python
import functools

import jax
import jax.numpy as jnp
from jax import lax
from jax.experimental import pallas as pl
from jax.experimental.pallas import tpu as pltpu

CHUNK = 64
WINDOW = 128
CONV_WIDTH = 31
TOP_K = 4
SWIGLU_LIMIT = 7.0
SWIGLU_ALPHA = 1.702
EPS = 1e-5
NEG_INF = -1e30

LANES = 128
SUBLANES = 8
STREAM_ROWS_PAD = 16
VMEM_LIMIT_BYTES = 56 * 1024 * 1024
ROW_TILE = 256
MM_ROW_CAP = 1536
MM_COL_TILE = 512
MOE_BLOCK = 512
MOE_GU_COL_TILE = 512
MOE_DN_COL_TILE = 1024
CONV_ROWS = 32
CONV_COL_CHUNK = 512
ATTN_ROWS = 256
ATTN_INTERLEAVE = 4
COMBINE_ROWS = 64
COMBINE_COL_CHUNK = 512
ADA_COL_TILE = 768

f32 = jnp.float32
bf16 = jnp.bfloat16


def _pick(total, cap, mult):
    best = None
    for t in range(mult, min(total, cap) + 1, mult):
        if total % t == 0:
            best = t
    assert best is not None, (total, cap, mult)
    return best


def _params(*sem):
    return pltpu.CompilerParams(dimension_semantics=sem, vmem_limit_bytes=VMEM_LIMIT_BYTES)


def _dot(a, b):
    return jnp.dot(a, b, preferred_element_type=f32)


def _ada_kernel(c_ref, w_ref, b_ref, o_ref):
    c = c_ref[...]
    s = (c * jax.nn.sigmoid(c)).astype(bf16)
    o_ref[...] = _dot(s, w_ref[...].astype(bf16)) + b_ref[...]


def _ada(c_pad, ada_w, ada_b):
    L, D, N = ada_w.shape
    R = c_pad.shape[0]
    tn = _pick(N, ADA_COL_TILE, LANES)
    return pl.pallas_call(
        _ada_kernel,
        grid=(L, N // tn),
        in_specs=[
            pl.BlockSpec((R, D), lambda l, n: (0, 0)),
            pl.BlockSpec((None, D, tn), lambda l, n: (l, 0, n)),
            pl.BlockSpec((None, 1, tn), lambda l, n: (l, 0, n)),
        ],
        out_specs=pl.BlockSpec((None, R, tn), lambda l, n: (l, 0, n)),
        out_shape=jax.ShapeDtypeStruct((L, R, N), f32),
        compiler_params=_params("arbitrary", "arbitrary"),
        name="ada",
    )(c_pad, ada_w, ada_b.reshape(L, 1, N))


def _modulated_norm(x, nw, scale, shift, rg):
    t, d = x.shape
    y = x * lax.rsqrt(jnp.mean(x * x, axis=-1, keepdims=True) + EPS) * nw
    h = y.reshape(t // rg, rg, d) * (1.0 + scale) + shift
    return h.reshape(t, d)


def _norm_kernel(x_ref, nw_ref, sc_ref, sh_ref, h_ref, *, rg):
    h = _modulated_norm(x_ref[...], nw_ref[...], sc_ref[...], sh_ref[...], rg)
    h_ref[...] = h.astype(h_ref.dtype)


def _norm_mod(x, nw, mod_g, shift_col, scale_col, rg, tile):
    T, D = x.shape
    gt = tile // rg
    return pl.pallas_call(
        functools.partial(_norm_kernel, rg=rg),
        grid=(T // tile,),
        in_specs=[
            pl.BlockSpec((tile, D), lambda i: (i, 0)),
            pl.BlockSpec((1, D), lambda i: (0, 0)),
            pl.BlockSpec((gt, 1, D), lambda i: (i, 0, scale_col)),
            pl.BlockSpec((gt, 1, D), lambda i: (i, 0, shift_col)),
        ],
        out_specs=pl.BlockSpec((tile, D), lambda i: (i, 0)),
        out_shape=jax.ShapeDtypeStruct((T, D), bf16),
        compiler_params=_params("arbitrary"),
        name="norm_mod",
    )(x, nw.reshape(1, D), mod_g, mod_g)


def _pack_bf16_pairs(h):
    d = h.shape[1]
    bits = pltpu.bitcast(h.astype(bf16).astype(f32), jnp.uint32)
    return (bits[:, : d // 2] >> 16) | (bits[:, d // 2:] & jnp.uint32(0xFFFF0000))


def _unpack_bf16_pairs(w):
    lo = pltpu.bitcast(w << 16, f32).astype(bf16)
    hi = pltpu.bitcast(w & jnp.uint32(0xFFFF0000), f32).astype(bf16)
    return lo, hi


def _norm_route_kernel(x_ref, nw_ref, sc_ref, sh_ref, rw_ref, rb_ref,
                       hp_ref, topi_ref, topw_ref, sel_ref, cnt_ref, *, rg, top_k):
    h = _modulated_norm(x_ref[...], nw_ref[...], sc_ref[...], sh_ref[...], rg)
    hp_ref[...] = _pack_bf16_pairs(h)
    logits = _dot(h.astype(bf16), rw_ref[...]) + rb_ref[...]
    t, e = logits.shape
    lane = lax.broadcasted_iota(jnp.int32, (t, e), 1).astype(f32)
    kcol = lax.broadcasted_iota(jnp.int32, (t, top_k), 1)
    sel = jnp.zeros((t, e), f32)
    topi = jnp.zeros((t, top_k), f32)
    topv = jnp.zeros((t, top_k), f32)
    work = logits
    for k in range(top_k):
        m = jnp.max(work, axis=-1, keepdims=True)
        idx = jnp.min(jnp.where(work == m, lane, float(e)), axis=-1, keepdims=True)
        hit = lane == idx
        sel = sel + hit.astype(f32)
        topi = jnp.where(kcol == k, idx, topi)
        topv = jnp.where(kcol == k, m, topv)
        work = jnp.where(hit, -jnp.inf, work)
    p = jnp.exp(topv - topv[:, 0:1])
    topw_ref[...] = p / jnp.sum(p, axis=-1, keepdims=True)
    topi_ref[...] = topi.astype(jnp.int32)
    sel_ref[...] = sel.astype(sel_ref.dtype)
    cnt_ref[...] = jnp.sum(sel, axis=0, keepdims=True)


def _norm_route(x, nw, mod_g, shift_col, scale_col, rw, rb, rg, tile):
    T, D = x.shape
    E = rw.shape[1]
    gt = tile // rg
    nt = T // tile
    return pl.pallas_call(
        functools.partial(_norm_route_kernel, rg=rg, top_k=TOP_K),
        grid=(nt,),
        in_specs=[
            pl.BlockSpec((tile, D), lambda i: (i, 0)),
            pl.BlockSpec((1, D), lambda i: (0, 0)),
            pl.BlockSpec((gt, 1, D), lambda i: (i, 0, scale_col)),
            pl.BlockSpec((gt, 1, D), lambda i: (i, 0, shift_col)),
            pl.BlockSpec((D, E), lambda i: (0, 0)),
            pl.BlockSpec((1, E), lambda i: (0, 0)),
        ],
        out_specs=[
            pl.BlockSpec((tile, D // 2), lambda i: (i, 0)),
            pl.BlockSpec((tile, TOP_K), lambda i: (i, 0)),
            pl.BlockSpec((tile, TOP_K), lambda i: (i, 0)),
            pl.BlockSpec((tile, E), lambda i: (i, 0)),
            pl.BlockSpec((None, 1, E), lambda i: (i, 0, 0)),
        ],
        out_shape=[
            jax.ShapeDtypeStruct((T, D // 2), jnp.uint32),
            jax.ShapeDtypeStruct((T, TOP_K), jnp.int32),
            jax.ShapeDtypeStruct((T, TOP_K), f32),
            jax.ShapeDtypeStruct((T, E), bf16),
            jax.ShapeDtypeStruct((nt, 1, E), f32),
        ],
        compiler_params=_params("arbitrary"),
        name="norm_route",
    )(x, nw.reshape(1, D), mod_g, mod_g, rw.astype(bf16), rb.reshape(1, E))


def _final_norm_kernel(x_ref, nw_ref, o_ref):
    x = x_ref[...]
    o_ref[...] = x * lax.rsqrt(jnp.mean(x * x, axis=-1, keepdims=True) + EPS) * nw_ref[...]


def _final_norm(x, nw, tile, row0, n_rows):
    D = x.shape[1]
    blk0 = row0 // tile
    return pl.pallas_call(
        _final_norm_kernel,
        grid=(n_rows // tile,),
        in_specs=[pl.BlockSpec((tile, D), lambda i: (blk0 + i, 0)), pl.BlockSpec((1, D), lambda i: (0, 0))],
        out_specs=pl.BlockSpec((tile, D), lambda i: (i, 0)),
        out_shape=jax.ShapeDtypeStruct((n_rows, D), f32),
        compiler_params=_params("arbitrary"),
        name="final_norm",
    )(x, nw.reshape(1, D))


def _glu_kernel(x_ref, wa_ref, wb_ref, ba_ref, bb_ref, o_ref):
    x = x_ref[...]
    a = _dot(x, wa_ref[...]) + ba_ref[...]
    b = _dot(x, wb_ref[...]) + bb_ref[...]
    o_ref[...] = a * jax.nn.sigmoid(b)


def _glu_matmul(x, w, b):
    T, K = x.shape
    N = w.shape[1] // 2
    tm = _pick(T, MM_ROW_CAP, 32)
    tn = _pick(N, MM_COL_TILE, LANES)
    nb = N // tn
    b2 = b.reshape(1, 2 * N)
    return pl.pallas_call(
        _glu_kernel,
        grid=(T // tm, nb),
        in_specs=[
            pl.BlockSpec((tm, K), lambda m, n: (m, 0)),
            pl.BlockSpec((K, tn), lambda m, n: (0, n)),
            pl.BlockSpec((K, tn), lambda m, n: (0, n + nb)),
            pl.BlockSpec((1, tn), lambda m, n: (0, n)),
            pl.BlockSpec((1, tn), lambda m, n: (0, n + nb)),
        ],
        out_specs=pl.BlockSpec((tm, tn), lambda m, n: (m, n)),
        out_shape=jax.ShapeDtypeStruct((T, N), f32),
        compiler_params=_params("arbitrary", "arbitrary"),
        name="pw1_glu",
    )(x, w, w, b2, b2)


def _linear_kernel(x_ref, w_ref, b_ref, o_ref):
    o_ref[...] = (_dot(x_ref[...], w_ref[...]) + b_ref[...]).astype(o_ref.dtype)


def _linear_grouped_out(x, w, b, groups, out_dtype, name):
    T, K = x.shape
    ng = w.shape[1] // groups
    tm = _pick(T, MM_ROW_CAP, 32)
    return pl.pallas_call(
        _linear_kernel,
        grid=(T // tm, groups),
        in_specs=[
            pl.BlockSpec((tm, K), lambda m, n: (m, 0)),
            pl.BlockSpec((K, ng), lambda m, n: (0, n)),
            pl.BlockSpec((1, ng), lambda m, n: (0, n)),
        ],
        out_specs=pl.BlockSpec((None, tm, ng), lambda m, n: (n, m, 0)),
        out_shape=jax.ShapeDtypeStruct((groups, T, ng), out_dtype),
        compiler_params=_params("arbitrary", "arbitrary"),
        name=name,
    )(x, w, b.reshape(1, -1))


def _linear(x, w, b, out_dtype, name):
    T, K = x.shape
    N = w.shape[1]
    tm = _pick(T, MM_ROW_CAP, 32)
    tn = _pick(N, MM_COL_TILE, LANES)
    return pl.pallas_call(
        _linear_kernel,
        grid=(T // tm, N // tn),
        in_specs=[
            pl.BlockSpec((tm, K), lambda m, n: (m, 0)),
            pl.BlockSpec((K, tn), lambda m, n: (0, n)),
            pl.BlockSpec((1, tn), lambda m, n: (0, n)),
        ],
        out_specs=pl.BlockSpec((tm, tn), lambda m, n: (m, n)),
        out_shape=jax.ShapeDtypeStruct((T, N), out_dtype),
        compiler_params=_params("arbitrary", "arbitrary"),
        name=name,
    )(x, w, b.reshape(1, -1))


def _residual_kernel(l_ref, w_ref, b_ref, xr_ref, g_ref, o_ref, *, rg):
    groups = l_ref.shape[0]
    acc = _dot(l_ref[0], w_ref[0])
    for g in range(1, groups):
        acc = acc + _dot(l_ref[g], w_ref[g])
    y = acc + b_ref[...]
    tm, tn = y.shape
    upd = (y.reshape(tm // rg, rg, tn) * g_ref[...]).reshape(tm, tn)
    o_ref[...] = xr_ref[...] + upd


def _residual_matmul(lhs, w, b, x_res, mod_g, gate_col, rg, name):
    G, T, Kg = lhs.shape
    N = w.shape[2]
    tm = _pick(T, MM_ROW_CAP, rg)
    tn = _pick(N, MM_COL_TILE, LANES)
    nb = N // tn
    return pl.pallas_call(
        functools.partial(_residual_kernel, rg=rg),
        grid=(T // tm, nb),
        in_specs=[
            pl.BlockSpec((G, tm, Kg), lambda m, n: (0, m, 0)),
            pl.BlockSpec((G, Kg, tn), lambda m, n: (0, 0, n)),
            pl.BlockSpec((1, tn), lambda m, n: (0, n)),
            pl.BlockSpec((tm, tn), lambda m, n: (m, n)),
            pl.BlockSpec((tm // rg, 1, tn), lambda m, n: (m, 0, gate_col * nb + n)),
        ],
        out_specs=pl.BlockSpec((tm, tn), lambda m, n: (m, n)),
        out_shape=jax.ShapeDtypeStruct((T, N), f32),
        compiler_params=_params("arbitrary", "arbitrary"),
        name=name,
    )(lhs, w, b.reshape(1, N), x_res, mod_g)


def _conv_kernel(g_ref, prev_ref, hist_ref, w_ref, b_ref, lg_ref, lb_ref, o_ref, phase, ybuf, *,
                 prompt_steps, steps_per_seq, col_chunk):
    tt, d = g_ref.shape
    hb = prev_ref.shape[0]
    taps = w_ref.shape[0]
    i = pl.program_id(0)
    prev = jnp.where((i % steps_per_seq) == 0, 0.0, prev_ref[...])
    phase[0, 0:hb, :] = jnp.where(i >= prompt_steps, hist_ref[...], prev)
    phase[0, hb:hb + tt, :] = g_ref[...]
    lo, hi = hb - (taps - 1), hb
    steps = {b: [a for a in range(hi // SUBLANES + 1) if lo <= SUBLANES * a + b <= hi] for b in range(SUBLANES)}
    for b in range(1, SUBLANES):
        if steps[b]:
            rows = SUBLANES * steps[b][-1] + tt
            phase[b, 0:rows, :] = phase[0, b:b + rows, :]
    for c in range(d // col_chunk):
        cs = slice(c * col_chunk, (c + 1) * col_chunk)
        acc = jnp.broadcast_to(b_ref[:, cs], (tt, col_chunk)).reshape(tt // SUBLANES, SUBLANES, col_chunk)
        for b in range(SUBLANES):
            if not steps[b]:
                continue
            a0 = steps[b][0]
            span = phase[b, SUBLANES * a0:SUBLANES * steps[b][-1] + tt, cs]
            for a in steps[b]:
                k = SUBLANES * a + b - lo
                rows = span[SUBLANES * (a - a0):SUBLANES * (a - a0) + tt, :]
                acc = acc + rows.reshape(tt // SUBLANES, SUBLANES, col_chunk) * w_ref[k, :, cs]
        ybuf[:, cs] = acc.reshape(tt, col_chunk)
    y = ybuf[...]
    mu = jnp.mean(y, axis=-1, keepdims=True)
    yc = y - mu
    var = jnp.mean(yc * yc, axis=-1, keepdims=True)
    yn = yc * lax.rsqrt(var + EPS) * lg_ref[...] + lb_ref[...]
    o_ref[...] = (yn * jax.nn.sigmoid(yn)).astype(o_ref.dtype)


def _conv_module(g, hist, dw_w, dw_b, ln_g, ln_b, n_prompt_rows, seq):
    T, D = g.shape
    tt = CONV_ROWS
    taps = dw_w.shape[0]
    prompt_steps = n_prompt_rows // tt
    last_hist = hist.shape[0] // tt - 1
    kern = functools.partial(_conv_kernel, prompt_steps=prompt_steps, steps_per_seq=seq // tt,
                             col_chunk=_pick(D, CONV_COL_CHUNK, LANES))
    vec = pl.BlockSpec((1, D), lambda i: (0, 0))
    return pl.pallas_call(
        kern,
        grid=(T // tt,),
        in_specs=[
            pl.BlockSpec((tt, D), lambda i: (i, 0)),
            pl.BlockSpec((tt, D), lambda i: (jnp.maximum(i - 1, 0), 0)),
            pl.BlockSpec((tt, D), lambda i: (jnp.clip(i - prompt_steps, 0, last_hist), 0)),
            pl.BlockSpec((taps, SUBLANES, D), lambda i: (0, 0, 0)),
            vec, vec, vec,
        ],
        out_specs=pl.BlockSpec((tt, D), lambda i: (i, 0)),
        out_shape=jax.ShapeDtypeStruct((T, D), bf16),
        scratch_shapes=[pltpu.VMEM((SUBLANES, 2 * tt, D), f32), pltpu.VMEM((tt, D), f32)],
        compiler_params=_params("arbitrary"),
        name="conv_ln_swish",
    )(g, g, hist, jnp.broadcast_to(dw_w[:, None, :], (taps, SUBLANES, D)),
      dw_b.reshape(1, D), ln_g.reshape(1, D), ln_b.reshape(1, D))


def _attn_prompt_kernel(*refs, live_tiles, **kw):
    o_ref = refs[7]
    live = pl.program_id(1) < live_tiles

    @pl.when(live)
    def _():
        _attn_kernel(*refs, **kw)

    @pl.when(jnp.logical_not(live))
    def _():
        o_ref[...] = jnp.zeros_like(o_ref)


def _attn_kernel(*refs, n_chunks, cq, has_halo, scale, chunks_per_seq, width=ATTN_INTERLEAVE):
    if has_halo:
        q_ref, km_ref, kh_ref, vm_ref, vh_ref, bias_ref, sink_ref, o_ref, kbuf, vbuf = refs
        hr = kh_ref.shape[0]
        kbuf[0:hr, :] = kh_ref[...]
        vbuf[0:hr, :] = vh_ref[...]
    else:
        q_ref, km_ref, vm_ref, bias_ref, sink_ref, o_ref, kbuf, vbuf = refs
        hr = 0
    mr = km_ref.shape[0]
    kbuf[hr:hr + mr, :] = km_ref[...]
    vbuf[hr:hr + mr, :] = vm_ref[...]
    G = q_ref.shape[0]
    pw = q_ref.shape[2]
    hd = pw // 2
    nb = bias_ref.shape[2]
    rows = G * cq
    lane_k = lax.broadcasted_iota(jnp.int32, (nb, pw), 1)
    lane_o = lax.broadcasted_iota(jnp.int32, (rows, pw), 1)
    col = lax.broadcasted_iota(jnp.int32, (rows, nb), 1)
    first_chunk = (pl.program_id(1) % (chunks_per_seq // n_chunks)) * n_chunks if has_halo else None
    heads = range(2)
    sinks = [sink_ref[hh] for hh in heads]
    for j0 in range(0, n_chunks, width):
        its = [(j, hh) for j in range(j0, min(j0 + width, n_chunks)) for hh in heads]
        vps = {j: vbuf[j * cq:j * cq + nb, :].astype(bf16) for j, _ in its}
        ss = []
        for j, hh in its:
            q = q_ref[:, j * cq:(j + 1) * cq, :].reshape(rows, pw)
            kp = kbuf[j * cq:j * cq + nb, :]
            in_head = (lane_k < hd) if hh == 0 else (lane_k >= hd)
            kh = jnp.where(in_head, kp, 0.0).astype(bf16)
            s = lax.dot_general(q, kh, (((1,), (1,)), ((), ())), preferred_element_type=f32)
            s = s * scale + bias_ref[hh]
            if has_halo:
                kpos0 = (first_chunk + j) * cq - (nb - cq)
                s = jnp.where(col + kpos0 >= 0, s, NEG_INF)
            ss.append(s)
        ms = [jnp.maximum(jnp.max(s, axis=-1, keepdims=True), sinks[hh]) for s, (j, hh) in zip(ss, its)]
        ps = [jnp.exp(s - m) for s, m in zip(ss, ms)]
        dens = [jnp.sum(p, axis=-1, keepdims=True) + jnp.exp(sinks[hh] - m) for p, m, (j, hh) in zip(ps, ms, its)]
        outs = [_dot(p.astype(bf16), vps[j]) / den for p, den, (j, hh) in zip(ps, dens, its)]
        for idx in range(0, len(its), 2):
            j = its[idx][0]
            o_pair = jnp.where(lane_o < hd, outs[idx], outs[idx + 1])
            o_ref[:, j * cq:(j + 1) * cq, :] = o_pair.reshape(G, cq, pw).astype(o_ref.dtype)


def _attn_tables(sinks, n_kv, group, cq, nb, window):
    n_heads = n_kv * group
    slopes = jnp.exp2(-8.0 * jnp.arange(1, n_heads + 1, dtype=f32) / n_heads).reshape(n_kv, group)
    qi = jnp.arange(cq, dtype=jnp.int32)[:, None]
    si = jnp.arange(nb, dtype=jnp.int32)[None, :]
    dist = jnp.abs(window + qi - si).astype(f32)
    bias = -slopes[:, :, None, None] * dist[None, None]
    bias = bias.reshape(n_kv, group * cq, nb)
    sink = jnp.broadcast_to(sinks.astype(f32).reshape(n_kv, group, 1, 1), (n_kv, group, cq, 1))
    return bias, sink.reshape(n_kv, group * cq, 1)


def _attn_prompt(q_r, kv, sinks, n_kv, hd, seq, n_rows, window, chunk):
    G, T, Q = q_r.shape
    pw = 2 * hd
    pairs = n_kv // 2
    tr = _pick(seq, ATTN_ROWS, max(chunk, window))
    n_chunks = tr // chunk
    nb = window + chunk
    bias, sink = _attn_tables(sinks, n_kv, G, chunk, nb, window)
    hpb = tr // window
    assert T % tr == 0 and n_rows % tr == 0
    kern = functools.partial(_attn_prompt_kernel, live_tiles=n_rows // tr, n_chunks=n_chunks, cq=chunk,
                             has_halo=True, scale=hd ** -0.5, chunks_per_seq=seq // chunk)
    halo = lambda off: (lambda p, i: (jnp.maximum(i * hpb - 1, 0), off + p))
    return pl.pallas_call(
        kern,
        grid=(pairs, T // tr),
        in_specs=[
            pl.BlockSpec((G, tr, pw), lambda p, i: (0, i, p)),
            pl.BlockSpec((tr, pw), lambda p, i: (i, p)),
            pl.BlockSpec((window, pw), halo(0)),
            pl.BlockSpec((tr, pw), lambda p, i: (i, pairs + p)),
            pl.BlockSpec((window, pw), halo(pairs)),
            pl.BlockSpec((2, G * chunk, nb), lambda p, i: (p, 0, 0)),
            pl.BlockSpec((2, G * chunk, 1), lambda p, i: (p, 0, 0)),
        ],
        out_specs=pl.BlockSpec((G, tr, pw), lambda p, i: (0, i, p)),
        out_shape=jax.ShapeDtypeStruct((G, T, Q), bf16),
        scratch_shapes=[pltpu.VMEM((window + tr, pw), f32), pltpu.VMEM((window + tr, pw), f32)],
        compiler_params=_params("arbitrary", "arbitrary"),
        name="attn_prompt",
    )(q_r, kv, kv, kv, kv, bias, sink)


def _attn_sample(q_r, kv_all, sinks, o_prev, n_kv, hd, t_new, row0, n_streams):
    G, T, Q = q_r.shape
    pw = 2 * hd
    pairs = n_kv // 2
    nb = kv_all.shape[0] // n_streams
    bias, sink = _attn_tables(sinks, n_kv, G, t_new, nb, nb - t_new)
    kern = functools.partial(_attn_kernel, n_chunks=1, cq=t_new, has_halo=False,
                             scale=hd ** -0.5, chunks_per_seq=1)
    blk0 = row0 // t_new
    return pl.pallas_call(
        lambda *refs: kern(*refs[:5], *refs[6:]),
        grid=(pairs, n_streams),
        in_specs=[
            pl.BlockSpec((G, t_new, pw), lambda p, i: (0, blk0 + i, p)),
            pl.BlockSpec((nb, pw), lambda p, i: (i, p)),
            pl.BlockSpec((nb, pw), lambda p, i: (i, pairs + p)),
            pl.BlockSpec((2, G * t_new, nb), lambda p, i: (p, 0, 0)),
            pl.BlockSpec((2, G * t_new, 1), lambda p, i: (p, 0, 0)),
            pl.BlockSpec(memory_space=pl.ANY),
        ],
        out_specs=pl.BlockSpec((G, t_new, pw), lambda p, i: (0, blk0 + i, p)),
        out_shape=jax.ShapeDtypeStruct((G, T, Q), bf16),
        scratch_shapes=[pltpu.VMEM((nb, pw), f32), pltpu.VMEM((nb, pw), f32)],
        input_output_aliases={5: 0},
        compiler_params=_params("arbitrary", "arbitrary"),
        name="attn_sample",
    )(q_r, kv_all, kv_all, bias, sink, o_prev)


def _pos_kernel(sel_ref, topi_ref, base_ref, dest_ref):
    t, e = sel_ref.shape
    top_k = topi_ref.shape[1]
    r = lax.broadcasted_iota(jnp.int32, (t, t), 0)
    c = lax.broadcasted_iota(jnp.int32, (t, t), 1)
    earlier = (r > c).astype(bf16)
    rank = _dot(earlier, sel_ref[...]) + base_ref[...]
    lane = lax.broadcasted_iota(jnp.int32, (t, e), 1)
    kcol = lax.broadcasted_iota(jnp.int32, (t, top_k), 1)
    topi = topi_ref[...]
    dest = jnp.zeros((t, top_k), f32)
    for k in range(top_k):
        dk = jnp.sum(jnp.where(lane == topi[:, k:k + 1], rank, 0.0), axis=-1, keepdims=True)
        dest = jnp.where(kcol == k, dk, dest)
    dest_ref[...] = dest.astype(jnp.int32)


def _positions(sel, topi, base, tile):
    T, E = sel.shape
    K = topi.shape[1]
    return pl.pallas_call(
        _pos_kernel,
        grid=(T // tile,),
        in_specs=[
            pl.BlockSpec((tile, E), lambda i: (i, 0)),
            pl.BlockSpec((tile, K), lambda i: (i, 0)),
            pl.BlockSpec((None, 1, E), lambda i: (i, 0, 0)),
        ],
        out_specs=pl.BlockSpec((tile, K), lambda i: (i, 0)),
        out_shape=jax.ShapeDtypeStruct((T, K), jnp.int32),
        compiler_params=_params("arbitrary"),
        name="moe_positions",
    )(sel, topi, base)


def _scatter_kernel(pe_ref, cn_ref, dest_ref, hp_ref, xs_ref, zbuf, sem, zsem, *, top_k, bm):
    tile = hp_ref.shape[0]
    n_exp = pe_ref.shape[0]

    n_blocks = xs_ref.shape[0] // bm
    n_used = pe_ref[n_exp - 1] // bm

    def zero_block(m):
        return pltpu.make_async_copy(zbuf, xs_ref.at[pl.ds(pl.multiple_of(m * bm, bm), bm)], zsem)

    def zero_tail(e):
        return zero_block(pe_ref[e] // bm - 1)

    @pl.when(pl.program_id(0) == 0)
    def _():
        zbuf[...] = jnp.zeros_like(zbuf)

        def start(e, c):
            @pl.when(cn_ref[e] > 0)
            def _():
                zero_tail(e).start()
            return c

        def wait(e, c):
            @pl.when(cn_ref[e] > 0)
            def _():
                zero_tail(e).wait()
            return c

        def start_unused(m, c):
            zero_block(m).start()
            return c

        def wait_unused(m, c):
            zero_block(m).wait()
            return c

        lax.fori_loop(0, n_exp, start, 0)
        lax.fori_loop(n_used, n_blocks, start_unused, 0)
        lax.fori_loop(0, n_exp, wait, 0)
        lax.fori_loop(n_used, n_blocks, wait_unused, 0)

    def row_copy(r, d):
        return pltpu.make_async_copy(hp_ref.at[pl.ds(r, 1)], xs_ref.at[pl.ds(d, 1)], sem)

    def issue(r, c):
        for k in range(top_k):
            row_copy(r, dest_ref[r * top_k + k]).start()
        return c

    def drain(r, c):
        for k in range(top_k):
            row_copy(r, dest_ref[r * top_k + k]).wait()
        return c

    lax.fori_loop(0, tile, issue, 0)
    lax.fori_loop(0, tile, drain, 0)


def _scatter_rows(hp, dest_flat, pad_end, counts, n_pad, tile, bm):
    T, W = hp.shape
    K = dest_flat.shape[0] // T
    grid_spec = pltpu.PrefetchScalarGridSpec(
        num_scalar_prefetch=2,
        grid=(T // tile,),
        in_specs=[
            pl.BlockSpec((tile * K,), lambda i, pe, cn: (i,), memory_space=pltpu.SMEM),
            pl.BlockSpec((tile, W), lambda i, pe, cn: (i, 0)),
        ],
        out_specs=pl.BlockSpec(memory_space=pl.ANY),
        scratch_shapes=[pltpu.VMEM((bm, W), jnp.uint32), pltpu.SemaphoreType.DMA, pltpu.SemaphoreType.DMA],
    )
    return pl.pallas_call(
        functools.partial(_scatter_kernel, top_k=K, bm=bm),
        grid_spec=grid_spec,
        out_shape=jax.ShapeDtypeStruct((n_pad, W), jnp.uint32),
        compiler_params=_params("arbitrary"),
        name="moe_scatter",
    )(pad_end, counts, dest_flat, hp)


def _expert_step(be_ref, nu_ref):
    m = pl.program_id(1)
    last = nu_ref[0] - 1
    mm = jnp.minimum(m, last)
    prev = jnp.maximum(mm - 1, 0)
    fresh = jnp.logical_or(m == 0, be_ref[mm] != be_ref[prev])
    return m <= last, fresh


def _moe_gu_kernel(be_ref, nu_ref, nx_ref, xs_ref, w_ref, bg_ref, bu_ref, o_ref, wg_bf, wu_bf, stage, sem, *,
                   layer, tn, n_col_tiles):
    live, fresh = _expert_step(be_ref, nu_ref)
    n = pl.program_id(0)
    m = pl.program_id(1)
    mm = jnp.minimum(m, nu_ref[0] - 1)
    ff = w_ref.shape[3] // 2

    def fetch(e, col_tile, part):
        col = pl.multiple_of(col_tile * tn, tn) + part * ff
        return pltpu.make_async_copy(w_ref.at[layer, e, :, pl.ds(col, tn)], stage.at[part], sem.at[part])

    @pl.when(jnp.logical_and(live, fresh))
    def _():
        @pl.when(jnp.logical_and(n == 0, m == 0))
        def _():
            fetch(be_ref[0], 0, 0).start()
            fetch(be_ref[0], 0, 1).start()

        fetch(0, 0, 0).wait()
        fetch(0, 0, 1).wait()
        wg_bf[...] = stage[0].astype(bf16)
        wu_bf[...] = stage[1].astype(bf16)
        nxt = nx_ref[mm]
        wraps = nxt < 0
        e_next = jnp.where(wraps, be_ref[0], nxt)
        n_next = jnp.where(wraps, n + 1, n)

        @pl.when(n_next < n_col_tiles)
        def _():
            fetch(e_next, n_next, 0).start()
            fetch(e_next, n_next, 1).start()

    @pl.when(live)
    def _():
        lo, hi = _unpack_bf16_pairs(xs_ref[...])
        half = lo.shape[1]
        gate = _dot(lo, wg_bf[0:half, :]) + _dot(hi, wg_bf[half:, :]) + bg_ref[...]
        up = _dot(lo, wu_bf[0:half, :]) + _dot(hi, wu_bf[half:, :]) + bu_ref[...]
        gate = jnp.minimum(gate, SWIGLU_LIMIT)
        up = jnp.clip(up, -SWIGLU_LIMIT, SWIGLU_LIMIT)
        act = (up + 1.0) * (gate * jax.nn.sigmoid(SWIGLU_ALPHA * gate))
        o_ref[...] = act.astype(o_ref.dtype)

    @pl.when(jnp.logical_not(live))
    def _():
        o_ref[...] = jnp.zeros_like(o_ref)


def _moe_gate_up(xs, blk_e, n_used, next_e, gu_w, gu_b, layer, bm):
    n_pad, W = xs.shape
    L, E, D, F2 = gu_w.shape
    F = F2 // 2
    tn = _pick(F, MOE_GU_COL_TILE, LANES)
    nb = F // tn
    row = lambda n, m, be, nu, nx: jnp.minimum(m, nu[0] - 1)
    exp = lambda n, m, be, nu, nx: be[jnp.minimum(m, nu[0] - 1)]
    grid_spec = pltpu.PrefetchScalarGridSpec(
        num_scalar_prefetch=3,
        grid=(nb, n_pad // bm),
        in_specs=[
            pl.BlockSpec((bm, W), lambda n, m, be, nu, nx: (row(n, m, be, nu, nx), 0)),
            pl.BlockSpec(memory_space=pl.ANY),
            pl.BlockSpec((None, None, 1, tn), lambda n, m, be, nu, nx: (layer, exp(n, m, be, nu, nx), 0, n)),
            pl.BlockSpec((None, None, 1, tn), lambda n, m, be, nu, nx: (layer, exp(n, m, be, nu, nx), 0, n + nb)),
        ],
        out_specs=pl.BlockSpec((bm, tn), lambda n, m, be, nu, nx: (m, n)),
        scratch_shapes=[pltpu.VMEM((D, tn), bf16), pltpu.VMEM((D, tn), bf16),
                        pltpu.VMEM((2, D, tn), f32), pltpu.SemaphoreType.DMA((2,))],
    )
    gu_b3 = gu_b.reshape(L, E, 1, F2)
    return pl.pallas_call(
        functools.partial(_moe_gu_kernel, layer=layer, tn=tn, n_col_tiles=nb),
        grid_spec=grid_spec,
        out_shape=jax.ShapeDtypeStruct((n_pad, F), bf16),
        compiler_params=_params("arbitrary", "arbitrary"),
        name="moe_gate_up",
    )(blk_e, n_used, next_e, xs, gu_w, gu_b3, gu_b3)


def _moe_dn_kernel(be_ref, nu_ref, nx_ref, a_ref, w_ref, b_ref, o_ref, w_bf, stage, sem, *, layer, tn, n_col_tiles):
    live, fresh = _expert_step(be_ref, nu_ref)
    n = pl.program_id(0)
    m = pl.program_id(1)
    mm = jnp.minimum(m, nu_ref[0] - 1)

    def fetch(e, col_tile):
        col = pl.multiple_of(col_tile * tn, tn)
        return pltpu.make_async_copy(w_ref.at[layer, e, :, pl.ds(col, tn)], stage, sem)

    @pl.when(jnp.logical_and(live, fresh))
    def _():
        @pl.when(jnp.logical_and(n == 0, m == 0))
        def _():
            fetch(be_ref[0], 0).start()

        fetch(0, 0).wait()
        w_bf[...] = stage[...].astype(bf16)
        nxt = nx_ref[mm]
        wraps = nxt < 0
        e_next = jnp.where(wraps, be_ref[0], nxt)
        n_next = jnp.where(wraps, n + 1, n)

        @pl.when(n_next < n_col_tiles)
        def _():
            fetch(e_next, n_next).start()

    @pl.when(live)
    def _():
        o_ref[...] = _dot(a_ref[...], w_bf[...]) + b_ref[...]

    @pl.when(jnp.logical_not(live))
    def _():
        o_ref[...] = jnp.zeros_like(o_ref)


def _moe_down(act, blk_e, n_used, next_e, dn_w, dn_b, layer, bm):
    n_pad, F = act.shape
    L, E, _, D = dn_w.shape
    tn = _pick(D, MOE_DN_COL_TILE, LANES)
    nb = D // tn
    row = lambda n, m, be, nu, nx: jnp.minimum(m, nu[0] - 1)
    exp = lambda n, m, be, nu, nx: be[jnp.minimum(m, nu[0] - 1)]
    grid_spec = pltpu.PrefetchScalarGridSpec(
        num_scalar_prefetch=3,
        grid=(nb, n_pad // bm),
        in_specs=[
            pl.BlockSpec((bm, F), lambda n, m, be, nu, nx: (row(n, m, be, nu, nx), 0)),
            pl.BlockSpec(memory_space=pl.ANY),
            pl.BlockSpec((None, None, 1, tn), lambda n, m, be, nu, nx: (layer, exp(n, m, be, nu, nx), 0, n)),
        ],
        out_specs=pl.BlockSpec((bm, tn), lambda n, m, be, nu, nx: (m, n)),
        scratch_shapes=[pltpu.VMEM((F, tn), bf16), pltpu.VMEM((F, tn), f32), pltpu.SemaphoreType.DMA],
    )
    return pl.pallas_call(
        functools.partial(_moe_dn_kernel, layer=layer, tn=tn, n_col_tiles=nb),
        grid_spec=grid_spec,
        out_shape=jax.ShapeDtypeStruct((n_pad, D), f32),
        compiler_params=_params("arbitrary", "arbitrary"),
        name="moe_down",
    )(blk_e, n_used, next_e, act, dn_w, dn_b.reshape(L, E, 1, D))


def _combine_kernel(dcur_ref, dnxt_ref, w_ref, x_ref, g_ref, ys_ref, o_ref, buf, sem, *, rg, n_tiles):
    tt, d = x_ref.shape
    top_k = w_ref.shape[1]
    i = pl.program_id(0)

    def row_copy(src_row, slot, k, r):
        return pltpu.make_async_copy(ys_ref.at[pl.ds(src_row, 1)], buf.at[slot, k, pl.ds(r, 1)], sem.at[slot])

    def issue(d_ref, slot):
        def body(r, c):
            for k in range(top_k):
                row_copy(d_ref[r * top_k + k], slot, k, r).start()
            return c
        lax.fori_loop(0, tt, body, 0)

    @pl.when(i == 0)
    def _():
        issue(dcur_ref, 0)

    @pl.when(i + 1 < n_tiles)
    def _():
        issue(dnxt_ref, (i + 1) % 2)

    slot = i % 2

    def drain(r, c):
        for k in range(top_k):
            row_copy(0, slot, k, r).wait()
        return c

    lax.fori_loop(0, tt, drain, 0)
    w = w_ref[...]
    cc = _pick(d, COMBINE_COL_CHUNK, LANES)
    for c in range(d // cc):
        cs = slice(c * cc, (c + 1) * cc)
        f = w[:, 0:1] * buf[slot, 0, :, cs]
        for k in range(1, top_k):
            f = f + w[:, k:k + 1] * buf[slot, k, :, cs]
        upd = (f.reshape(tt // rg, rg, cc) * g_ref[:, :, cs]).reshape(tt, cc)
        o_ref[:, cs] = x_ref[:, cs] + upd


def _combine(ys, dest_flat, topw, x, mod_g, gate_col, rg):
    T, D = x.shape
    K = topw.shape[1]
    tt = _pick(T, COMBINE_ROWS, rg)
    n_tiles = T // tt
    return pl.pallas_call(
        functools.partial(_combine_kernel, rg=rg, n_tiles=n_tiles),
        grid=(n_tiles,),
        in_specs=[
            pl.BlockSpec((tt * K,), lambda i: (i,), memory_space=pltpu.SMEM),
            pl.BlockSpec((tt * K,), lambda i: (jnp.minimum(i + 1, n_tiles - 1),), memory_space=pltpu.SMEM),
            pl.BlockSpec((tt, K), lambda i: (i, 0)),
            pl.BlockSpec((tt, D), lambda i: (i, 0)),
            pl.BlockSpec((tt // rg, 1, D), lambda i: (i, 0, gate_col)),
            pl.BlockSpec(memory_space=pl.ANY),
        ],
        out_specs=pl.BlockSpec((tt, D), lambda i: (i, 0)),
        out_shape=jax.ShapeDtypeStruct((T, D), f32),
        scratch_shapes=[pltpu.VMEM((2, K, tt, D), f32), pltpu.SemaphoreType.DMA((2,))],
        compiler_params=_params("arbitrary"),
        name="moe_combine",
    )(dest_flat, dest_flat, topw, x, mod_g, ys)


def _moe_layer(x, nw, mod_g, rw, rb, gu_w, gu_b, dn_w, dn_b, layer, rg, tile):
    T, D = x.shape
    E = rw.shape[1]
    bm = MOE_BLOCK
    hp, topi, topw, sel, cnt = _norm_route(x, nw, mod_g, 3, 4, rw, rb, rg, tile)
    cnt_i = cnt[:, 0, :].astype(jnp.int32)
    counts = jnp.sum(cnt_i, axis=0)
    padded = (counts + bm - 1) // bm * bm
    pad_end = jnp.cumsum(padded)
    pad_start = pad_end - padded
    base = (pad_start[None, :] + jnp.cumsum(cnt_i, axis=0) - cnt_i).astype(f32)[:, None, :]
    n_blocks = -(-(T * TOP_K + E * (bm - 1)) // bm)
    blk_start = jnp.arange(n_blocks, dtype=jnp.int32) * bm
    blk_e = jnp.sum((pad_end[None, :] <= blk_start[:, None]).astype(jnp.int32), axis=1)
    blk_e = jnp.minimum(blk_e, E - 1)
    n_used = (pad_end[-1:] // bm).astype(jnp.int32)
    ids = jnp.arange(E, dtype=jnp.int32)
    later = (counts[None, :] > 0) & (ids[None, :] > ids[:, None])
    next_nonempty = jnp.min(jnp.where(later, ids[None, :], E), axis=1)
    next_e = jnp.where(next_nonempty == E, -1, next_nonempty)[blk_e].astype(jnp.int32)
    dest = _positions(sel, topi, base, tile).reshape(T * TOP_K)
    xs = _scatter_rows(hp, dest, pad_end.astype(jnp.int32), counts, n_blocks * bm, tile, bm)
    act = _moe_gate_up(xs, blk_e, n_used, next_e, gu_w, gu_b, layer, bm)
    ys = _moe_down(act, blk_e, n_used, next_e, dn_w, dn_b, layer, bm)
    return _combine(ys, dest, topw, x, mod_g, 5, rg)


def _tails(a, row0, n_seq, seq_len, keep):
    return jnp.stack([a[row0 + (s + 1) * seq_len - keep:row0 + (s + 1) * seq_len] for s in range(n_seq)])


def kernel(x_prompt, x_sample, c_prompt, c_sample, state_conv, cache_k, cache_v, ada_w, ada_b, norm_mix, norm_ffn, norm_final, conv_pw1_w, conv_pw1_b, conv_dw_w, conv_dw_b, conv_ln_g, conv_ln_b, conv_pw2_w, conv_pw2_b, attn_qkv_w, attn_qkv_b, attn_o_w, attn_o_b, attn_sinks, router_w, router_b, moe_gu_w, moe_gu_b, moe_dn_w, moe_dn_b):
    B, S, D = x_prompt.shape
    Bs, Ts, _ = x_sample.shape
    depth = ada_w.shape[0]
    n_kv, hd = cache_k.shape[3], cache_k.shape[4]
    n_heads = attn_sinks.shape[1]
    group = n_heads // n_kv
    w_cache = cache_k.shape[2]
    Tp, Tsm = B * S, Bs * Ts
    T = Tp + Tsm
    rg = Ts
    assert S % rg == 0 and rg % 8 == 0 and rg == CONV_ROWS and n_kv % 2 == 0 and 2 * hd == LANES
    tile = _pick(Tsm, ROW_TILE, rg)
    assert Tp % tile == 0

    x = jnp.concatenate([x_prompt.reshape(Tp, D), x_sample.reshape(Tsm, D)], axis=0)
    n_streams = B + Bs
    c_all = jnp.concatenate([c_prompt, c_sample], axis=0)
    c_pad = jnp.pad(c_all, ((0, -n_streams % STREAM_ROWS_PAD), (0, 0)))
    mod = _ada(c_pad, ada_w, ada_b)
    stream_of_group = jnp.concatenate([
        jnp.repeat(jnp.arange(B, dtype=jnp.int32), S // rg),
        B + jnp.repeat(jnp.arange(Bs, dtype=jnp.int32), Ts // rg)])
    conv_states, k_states, v_states = [], [], []
    for i in range(depth):
        mod_g = mod[i][stream_of_group][:, None, :]
        j = i // 2
        h = _norm_mod(x, norm_mix[i], mod_g, 0, 1, rg, tile)
        if i % 2 == 0:
            g = _glu_matmul(h, conv_pw1_w[j].astype(bf16), conv_pw1_b[j])
            hist = jnp.pad(state_conv[j], ((0, 0), (rg - (CONV_WIDTH - 1), 0), (0, 0))).reshape(Bs * rg, D)
            a = _conv_module(g, hist, conv_dw_w[j], conv_dw_b[j], conv_ln_g[j], conv_ln_b[j], Tp, S)
            x = _residual_matmul(a[None], conv_pw2_w[j].astype(bf16)[None], conv_pw2_b[j], x, mod_g, 2, rg,
                                 "pw2_residual")
            keep = CONV_WIDTH - 1
            conv_states.append((_tails(g, 0, B, S, keep), _tails(g, Tp, Bs, Ts, keep)))
        else:
            qd = n_heads * hd
            kd = n_kv * hd
            wq = attn_qkv_w[j][:, :qd].reshape(D, n_kv, group, hd).transpose(0, 2, 1, 3).reshape(D, qd)
            bq = attn_qkv_b[j][:qd].reshape(n_kv, group, hd).transpose(1, 0, 2).reshape(qd)
            q_r = _linear_grouped_out(h, wq.astype(bf16), bq, group, bf16, "q_proj")
            kv = _linear(h, attn_qkv_w[j][:, qd:].astype(bf16), attn_qkv_b[j][qd:], f32, "kv_proj")
            o_r = _attn_prompt(q_r, kv, attn_sinks[j], n_kv, hd, S, Tp, WINDOW, CHUNK)
            kv_new = kv[Tp:].reshape(Bs, Ts, 2 * kd)
            cache = jnp.concatenate([cache_k[j].reshape(Bs, w_cache, kd), cache_v[j].reshape(Bs, w_cache, kd)], axis=-1)
            kv_all = jnp.concatenate([cache, kv_new], axis=1)
            o_r = _attn_sample(q_r, kv_all.reshape(Bs * (w_cache + Ts), 2 * kd), attn_sinks[j], o_r,
                               n_kv, hd, Ts, Tp, Bs)
            wo = attn_o_w[j].reshape(n_kv, group, hd, D).transpose(1, 0, 2, 3).reshape(group, kd, D)
            x = _residual_matmul(o_r, wo.astype(bf16), attn_o_b[j], x, mod_g, 2, rg, "o_residual")
            kvp = _tails(kv, 0, B, S, WINDOW).reshape(B, WINDOW, 2, n_kv, hd)
            kvs = kv_all[:, Ts:].reshape(Bs, w_cache, 2, n_kv, hd)
            k_states.append((kvp[:, :, 0], kvs[:, :, 0]))
            v_states.append((kvp[:, :, 1], kvs[:, :, 1]))
        x = _moe_layer(x, norm_ffn[i], mod_g, router_w[i], router_b[i], moe_gu_w, moe_gu_b,
                       moe_dn_w, moe_dn_b, i, rg, tile)
    y_prompt = _final_norm(x, norm_final, tile, 0, Tp)
    y_sample = _final_norm(x, norm_final, tile, Tp, Tsm)
    stack = lambda states, idx: jnp.stack([s[idx] for s in states])
    return (y_prompt.reshape(B, S, D), y_sample.reshape(Bs, Ts, D),
            stack(conv_states, 0), stack(conv_states, 1),
            stack(k_states, 0), stack(v_states, 0), stack(k_states, 1), stack(v_states, 1))
```
